```python
import jax, jax.numpy as jnp
from jax import lax
import numpy as np

D_MODEL = 1024
BATCH = 2
SEQ = 16384
DEPTH = 2

GRID_W = 64
CTX_LEN = 256
HEAD_DIM = 64
N_HEADS = 8
N_KV_HEADS = 2
D_ATTN = N_HEADS * HEAD_DIM
D_CONV_A = 256
D_CONF = 256
D_MIX = D_CONV_A + D_ATTN + D_CONF
SHORT_CONV_W = 3
CONF_CONV_W = 31
WINDOW = 128
BLOCK = 128
D_FF = 4 * D_MODEL
ROPE_BASE = 10000.0
EPS = 1e-6
NEG_INF = -1e30

OFF_Q = 3 * D_CONV_A
OFF_K = OFF_Q + D_ATTN
OFF_V = OFF_K + N_KV_HEADS * HEAD_DIM
OFF_C = OFF_V + N_KV_HEADS * HEAD_DIM
D_IN = OFF_C + 2 * D_CONF

kernel_name = "hybrid_parallel_groups_diffusion_block"


def rms_norm(x, g):
    x32 = x.astype(jnp.float32)
    y = x32 * lax.rsqrt(jnp.mean(x32 * x32, axis=-1, keepdims=True) + EPS)
    return y.astype(x.dtype) * g


def layer_norm(x, g, b):
    x32 = x.astype(jnp.float32)
    mu = jnp.mean(x32, axis=-1, keepdims=True)
    xc = x32 - mu
    y = xc * lax.rsqrt(jnp.mean(xc * xc, axis=-1, keepdims=True) + EPS)
    return y.astype(x.dtype) * g + b


def modulate(h, shift, scale):
    return h * (1 + scale) + shift


def heads(t, n):
    return t.reshape(t.shape[:-1] + (n, HEAD_DIM))


def depthwise_conv(x, w, b=None):
    k = w.shape[0]
    y = lax.conv_general_dilated(
        x, w[:, None, :], window_strides=(1,), padding=[(k // 2, k // 2)],
        dimension_numbers=('NWC', 'WIO', 'NWC'), feature_group_count=x.shape[-1])
    return y if b is None else y + b


def axial_rope(t, row, col):
    d_axis = HEAD_DIM // 2
    half = d_axis // 2
    inv_freq = ROPE_BASE ** (-jnp.arange(0, d_axis, 2, dtype=jnp.float32) / d_axis)

    def rot(u, pos):
        ang = pos.astype(jnp.float32)[:, None] * inv_freq[None, :]
        cos = jnp.cos(ang)[None, :, None, :].astype(u.dtype)
        sin = jnp.sin(ang)[None, :, None, :].astype(u.dtype)
        u1, u2 = u[..., :half], u[..., half:]
        return jnp.concatenate([u1 * cos - u2 * sin, u1 * sin + u2 * cos], axis=-1)

    return jnp.concatenate([rot(t[..., :d_axis], row), rot(t[..., d_axis:], col)], axis=-1)


def window_attention(q, k, v, kc, vc, sink):
    B, S, H, dh = q.shape
    KVH = k.shape[2]
    G = H // KVH
    nb = S // BLOCK
    n_ctx = kc.shape[1]
    scale = dh ** -0.5
    qb = q.reshape(B, nb, BLOCK, KVH, G, dh)

    def band(t):
        tp = jnp.pad(t, ((0, 0), (BLOCK, BLOCK), (0, 0), (0, 0))).reshape(B, nb + 2, BLOCK, KVH, dh)
        return jnp.concatenate([tp[:, :-2], tp[:, 1:-1], tp[:, 2:]], axis=2)

    kw, vw = band(k), band(v)
    s_loc = jnp.einsum('bnqkgd,bnskd->bnkgqs', qb, kw).astype(jnp.float32) * scale
    qi = jnp.arange(BLOCK)
    si = jnp.arange(3 * BLOCK)
    rel = si[None, :] - BLOCK - qi[:, None]
    kpos = jnp.arange(nb)[:, None] * BLOCK - BLOCK + si[None, :]
    valid = (jnp.abs(rel) <= WINDOW)[None] & ((kpos >= 0) & (kpos < S))[:, None, :]
    s_loc = jnp.where(valid[None, :, None, None], s_loc, NEG_INF)
    s_ctx = jnp.einsum('bnqkgd,bckd->bnkgqc', qb, kc).astype(jnp.float32) * scale
    sink_col = jnp.broadcast_to(sink.astype(jnp.float32).reshape(1, 1, KVH, G, 1, 1),
                                s_loc.shape[:-1] + (1,))
    p = jax.nn.softmax(jnp.concatenate([s_loc, s_ctx, sink_col], axis=-1), axis=-1).astype(v.dtype)
    n_loc = 3 * BLOCK
    o = (jnp.einsum('bnkgqs,bnskd->bnqkgd', p[..., :n_loc], vw)
         + jnp.einsum('bnkgqc,bckd->bnqkgd', p[..., n_loc:n_loc + n_ctx], vc))
    return o.reshape(B, S, H * dh)


def context_attention(q, k, v, sink):
    B, C, H, dh = q.shape
    KVH = k.shape[2]
    G = H // KVH
    qg = q.reshape(B, C, KVH, G, dh)
    s = jnp.einsum('bqkgd,bckd->bkgqc', qg, k).astype(jnp.float32) * (dh ** -0.5)
    sink_col = jnp.broadcast_to(sink.astype(jnp.float32).reshape(1, KVH, G, 1, 1), s.shape[:-1] + (1,))
    p = jax.nn.softmax(jnp.concatenate([s, sink_col], axis=-1), axis=-1).astype(v.dtype)
    o = jnp.einsum('bkgqc,bckd->bqkgd', p[..., :-1], v)
    return o.reshape(B, C, H * dh)


def short_conv_mix(u, w):
    x_in, b_gate, c_gate = jnp.split(u, 3, axis=-1)
    return b_gate * depthwise_conv(c_gate * x_in, w)


def conformer_conv(u, w, b, g, beta):
    val, gate = jnp.split(u, 2, axis=-1)
    y = depthwise_conv(val * jax.nn.sigmoid(gate), w, b)
    return jax.nn.silu(layer_norm(y, g, beta))


def sq_relu_mlp(h, w1, w2):
    return jnp.square(jax.nn.relu(h @ w1)) @ w2


def setup_inputs(seed: int = 0) -> dict:
    key = jax.random.key(seed)
    ks = jax.random.split(key, 24)
    L = DEPTH

    def nrm(k, shape, s):
        return jax.random.normal(k, shape, jnp.float32) * s

    return {
        "x": nrm(ks[0], (BATCH, SEQ, D_MODEL), 1.0),
        "c": nrm(ks[1], (BATCH, D_MODEL), 1.0),
        "ctx": nrm(ks[2], (BATCH, CTX_LEN, D_MODEL), 1.0),
        "c_ctx": nrm(ks[3], (D_MODEL,), 1.0),
        "w_mod": nrm(ks[4], (L, D_MODEL, 6 * D_MODEL), 0.5 * D_MODEL ** -0.5),
        "b_mod": nrm(ks[5], (L, 6 * D_MODEL), 0.02),
        "norm1_g": 1.0 + nrm(ks[6], (L, D_MODEL), 0.05),
        "w_in": nrm(ks[7], (L, D_MODEL, D_IN), D_MODEL ** -0.5),
        "conv_a_w": nrm(ks[8], (L, SHORT_CONV_W, D_CONV_A), SHORT_CONV_W ** -0.5),
        "q_norm_g": 1.0 + nrm(ks[9], (L, HEAD_DIM), 0.05),
        "k_norm_g": 1.0 + nrm(ks[10], (L, HEAD_DIM), 0.05),
        "attn_sink": nrm(ks[11], (L, N_HEADS), 0.5),
        "conv_c_w": nrm(ks[12], (L, CONF_CONV_W, D_CONF), CONF_CONV_W ** -0.5),
        "conv_c_b": nrm(ks[13], (L, D_CONF), 0.02),
        "ln_c_g": 1.0 + nrm(ks[14], (L, D_CONF), 0.05),
        "ln_c_b": nrm(ks[15], (L, D_CONF), 0.02),
        "w_out": nrm(ks[16], (L, D_MIX, D_MODEL), D_MIX ** -0.5),
        "norm2_g": 1.0 + nrm(ks[17], (L, D_MODEL), 0.05),
        "w_mlp1": nrm(ks[18], (L, D_MODEL, D_FF), D_MODEL ** -0.5),
        "w_mlp2": nrm(ks[19], (L, D_FF, D_MODEL), D_FF ** -0.5),
    }


def reference(x, c, ctx, c_ctx, w_mod, b_mod, norm1_g, w_in, conv_a_w, q_norm_g, k_norm_g,
              attn_sink, conv_c_w, conv_c_b, ln_c_g, ln_c_b, w_out, norm2_g, w_mlp1, w_mlp2):
    S = x.shape[1]
    ROWS = S // GRID_W
    row = jnp.repeat(jnp.arange(ROWS, dtype=jnp.int32), GRID_W)
    col = jnp.tile(jnp.arange(GRID_W, dtype=jnp.int32), ROWS)
    xc = ctx
    silu_c = jax.nn.silu(c)
    silu_cc = jax.nn.silu(c_ctx)

    for i in range(DEPTH):
        last = i == DEPTH - 1
        mod_l = (silu_c @ w_mod[i] + b_mod[i])[:, None, :]
        sh1, sc1, g1, sh2, sc2, g2 = jnp.split(mod_l, 6, axis=-1)
        mod_c = silu_cc @ w_mod[i] + b_mod[i]
        csh1, csc1, cg1, csh2, csc2, cg2 = jnp.split(mod_c, 6, axis=-1)

        hc = modulate(rms_norm(xc, norm1_g[i]), csh1, csc1)
        if last:
            uc_kv = hc @ w_in[i][:, OFF_K:OFF_C]
            kc_raw, vc_raw = uc_kv[..., :OFF_V - OFF_K], uc_kv[..., OFF_V - OFF_K:]
        else:
            uc = hc @ w_in[i]
            kc_raw, vc_raw = uc[..., OFF_K:OFF_V], uc[..., OFF_V:OFF_C]
        kc = rms_norm(heads(kc_raw, N_KV_HEADS), k_norm_g[i])
        vc = heads(vc_raw, N_KV_HEADS)

        h = modulate(rms_norm(x, norm1_g[i]), sh1, sc1)
        u = h @ w_in[i]
        q = axial_rope(rms_norm(heads(u[..., OFF_Q:OFF_K], N_HEADS), q_norm_g[i]), row, col)
        k = axial_rope(rms_norm(heads(u[..., OFF_K:OFF_V], N_KV_HEADS), k_norm_g[i]), row, col)
        v = heads(u[..., OFF_V:OFF_C], N_KV_HEADS)
        y = jnp.concatenate([
            short_conv_mix(u[..., :OFF_Q], conv_a_w[i]),
            window_attention(q, k, v, kc, vc, attn_sink[i]),
            conformer_conv(u[..., OFF_C:], conv_c_w[i], conv_c_b[i], ln_c_g[i], ln_c_b[i]),
        ], axis=-1)
        x = x + g1 * (y @ w_out[i])
        x = x + g2 * sq_relu_mlp(modulate(rms_norm(x, norm2_g[i]), sh2, sc2), w_mlp1[i], w_mlp2[i])

        if not last:
            qc = rms_norm(heads(uc[..., OFF_Q:OFF_K], N_HEADS), q_norm_g[i])
            yc = jnp.concatenate([
                short_conv_mix(uc[..., :OFF_Q], conv_a_w[i]),
                context_attention(qc, kc, vc, attn_sink[i]),
                conformer_conv(uc[..., OFF_C:], conv_c_w[i], conv_c_b[i], ln_c_g[i], ln_c_b[i]),
            ], axis=-1)
            xc = xc + cg1 * (yc @ w_out[i])
            xc = xc + cg2 * sq_relu_mlp(modulate(rms_norm(xc, norm2_g[i]), csh2, csc2),
                                        w_mlp1[i], w_mlp2[i])
    return x
```

```python
import functools

import jax
import jax.numpy as jnp
from jax import lax
from jax.experimental import pallas as pl
from jax.experimental.pallas import tpu as pltpu

F32 = jnp.float32
BF16 = jnp.bfloat16

D_MODEL = 1024
HEAD_DIM = 64
N_HEADS = 8
N_KV = 2
GROUP = N_HEADS // N_KV
D_CONV = 256
D_ATTN = N_HEADS * HEAD_DIM
D_FF = 4 * D_MODEL
GRID_W = 64
WINDOW = 128
SHORT_W = 3
CONF_W = 31
HALO = 16
ROPE_BASE = 10000.0
EPS = 1e-6
NEG_INF = -1e30

OFF_Q = 3 * D_CONV
OFF_K = OFF_Q + D_ATTN
OFF_C = OFF_K + 2 * N_KV * HEAD_DIM
D_IN = OFF_C + 2 * D_CONV
Y_ATTN = D_CONV
Y_CONF = D_CONV + D_ATTN

LANES = 128
TOKEN_BLOCK = 512
FF_CHUNK = 1024
MOD_COLS = 1536
VMEM_LIMIT = 56 * 1024 * 1024


def _sigmoid(t):
    return 1.0 / (1.0 + jnp.exp(-t))


def _dot(a, b):
    return jnp.dot(a, b, preferred_element_type=F32)


def _dot_nt(a, b):
    return lax.dot_general(a, b, (((1,), (1,)), ((), ())), preferred_element_type=F32)


def _mod_kernel(c_ref, w_ref, b_ref, o_ref):
    c = c_ref[...]
    s = c * _sigmoid(c)
    o_ref[0] = jnp.dot(s, w_ref[0], preferred_element_type=F32,
                       precision=lax.Precision.HIGHEST) + b_ref[0]


def _modulation(cvec, w_mod, b_mod):
    depth, _, n = w_mod.shape
    return pl.pallas_call(
        _mod_kernel,
        grid=(depth, n // MOD_COLS),
        in_specs=[
            pl.BlockSpec((8, D_MODEL), lambda l, j: (0, 0)),
            pl.BlockSpec((1, D_MODEL, MOD_COLS), lambda l, j: (l, 0, j)),
            pl.BlockSpec((1, 1, MOD_COLS), lambda l, j: (l, 0, j)),
        ],
        out_specs=pl.BlockSpec((1, 8, MOD_COLS), lambda l, j: (l, 0, j)),
        out_shape=jax.ShapeDtypeStruct((depth, 8, n), F32),
        compiler_params=pltpu.CompilerParams(
            dimension_semantics=("arbitrary", "arbitrary"), vmem_limit_bytes=VMEM_LIMIT),
        name="modulation",
    )(cvec, w_mod, b_mod.reshape(depth, 1, n))


def _proj_kernel(x_ref, sh_ref, sc_ref, g_ref, w_ref, cos_ref, sin_ref, gq_ref, gk_ref,
                 cf_ref, q_ref, kv_ref):
    x = x_ref[0]
    ms = jnp.mean(x * x, axis=-1, keepdims=True)
    h = (x * lax.rsqrt(ms + EPS)) * g_ref[...]
    h = h * (1.0 + sc_ref[0]) + sh_ref[0]
    hb = h.astype(BF16)

    ua = _dot(hb, w_ref[:, 0:OFF_Q])
    cf_ref[0, :, 0:D_CONV] = ua[:, 2 * D_CONV:3 * D_CONV] * ua[:, 0:D_CONV]
    cf_ref[0, :, 2 * D_CONV:3 * D_CONV] = ua[:, D_CONV:2 * D_CONV]

    uc = _dot(hb, w_ref[:, OFF_C:D_IN])
    cf_ref[0, :, D_CONV:2 * D_CONV] = uc[:, 0:D_CONV] * _sigmoid(uc[:, D_CONV:2 * D_CONV])

    lane = lax.broadcasted_iota(jnp.int32, (1, LANES), 1)
    rot_from_above = (lane % (HEAD_DIM // 2)) < (HEAD_DIM // 4)
    low_head = lane < HEAD_DIM
    r_i = lax.broadcasted_iota(jnp.int32, (LANES, LANES), 0) // HEAD_DIM
    c_i = lax.broadcasted_iota(jnp.int32, (LANES, LANES), 1) // HEAD_DIM
    head_ones = jnp.where(r_i == c_i, 1.0, 0.0).astype(BF16)
    cos = cos_ref[...]
    sin = sin_ref[...]

    def norm_rope(t, gain, scale):
        ssq = _dot((t * t).astype(BF16), head_ones)
        rs = lax.rsqrt(ssq * (1.0 / HEAD_DIM) + EPS) * scale
        tg = t * gain
        partner = jnp.where(rot_from_above, pltpu.roll(tg, LANES - HEAD_DIM // 4, 1),
                            pltpu.roll(tg, HEAD_DIM // 4, 1))
        return rs * (tg * cos + partner * sin)

    uq = _dot(hb, w_ref[:, OFF_Q:OFF_K])
    gq = gq_ref[...]
    for s in range(D_ATTN // LANES):
        qs = norm_rope(uq[:, s * LANES:(s + 1) * LANES], gq, HEAD_DIM ** -0.5)
        q_ref[0, :, s * LANES:(s + 1) * LANES] = qs.astype(BF16)

    ukv = _dot(hb, w_ref[:, OFF_K:OFF_C])
    kn = norm_rope(ukv[:, 0:LANES], gk_ref[...], 1.0)
    for j, t in enumerate((kn, ukv[:, LANES:2 * LANES])):
        swapped = pltpu.roll(t, HEAD_DIM, 1)
        kv_ref[0, :, (2 * j) * LANES:(2 * j + 1) * LANES] = jnp.where(low_head, t, swapped).astype(BF16)
        kv_ref[0, :, (2 * j + 1) * LANES:(2 * j + 2) * LANES] = jnp.where(low_head, swapped, t).astype(BF16)


def _proj(x, mods, mod_row, norm_g, w_in_b, cos_t, sin_t, gq, gk, tb):
    b, n, _ = x.shape
    const = lambda bb, i: (0, 0)
    return pl.pallas_call(
        _proj_kernel,
        grid=(b, n // tb),
        in_specs=[
            pl.BlockSpec((1, tb, D_MODEL), lambda bb, i: (bb, i, 0)),
            pl.BlockSpec((1, 1, D_MODEL), lambda bb, i: (mod_row(bb), 0, 0)),
            pl.BlockSpec((1, 1, D_MODEL), lambda bb, i: (mod_row(bb), 0, 1)),
            pl.BlockSpec((1, D_MODEL), const),
            pl.BlockSpec((D_MODEL, D_IN), const),
            pl.BlockSpec((tb, LANES), lambda bb, i: (i, 0)),
            pl.BlockSpec((tb, LANES), lambda bb, i: (i, 0)),
            pl.BlockSpec((1, LANES), const),
            pl.BlockSpec((1, LANES), const),
        ],
        out_specs=[
            pl.BlockSpec((1, tb, 3 * D_CONV), lambda bb, i: (bb, i, 0)),
            pl.BlockSpec((1, tb, D_ATTN), lambda bb, i: (bb, i, 0)),
            pl.BlockSpec((1, tb, 4 * LANES), lambda bb, i: (bb, i, 0)),
        ],
        out_shape=[
            jax.ShapeDtypeStruct((b, n, 3 * D_CONV), F32),
            jax.ShapeDtypeStruct((b, n, D_ATTN), BF16),
            jax.ShapeDtypeStruct((b, n, 4 * LANES), BF16),
        ],
        compiler_params=pltpu.CompilerParams(
            dimension_semantics=("parallel", "parallel"), vmem_limit_bytes=VMEM_LIMIT),
        name="proj",
    )(x, mods, mods, norm_g, w_in_b, cos_t, sin_t, gq, gk)


def _mix_kernel(local, tb, seq,
                x_ref, cf_ref, cfp_ref, cfn_ref, q_ref, kv_ref, kvp_ref, kvn_ref, ckv_ref,
                wa_ref, wc_ref, cb_ref, lg_ref, lb_ref, sink_ref, wo_ref, g1_ref,
                o_ref, cbuf, kvbuf, ybuf):
    i = pl.program_id(1)
    has_prev = i > 0
    has_next = i < pl.num_programs(1) - 1

    cbuf[0:HALO, :] = jnp.where(has_prev, cfp_ref[0, :, 0:2 * D_CONV], 0.0)
    cbuf[HALO:HALO + tb, :] = cf_ref[0, :, 0:2 * D_CONV]
    cbuf[HALO + tb:HALO + tb + HALO, :] = jnp.where(has_next, cfn_ref[0, :, 0:2 * D_CONV], 0.0)

    rows = 128
    for c in range(tb // rows):
        r0 = c * rows
        a = None
        for k in range(SHORT_W):
            s0 = HALO + r0 + k - SHORT_W // 2
            term = cbuf[s0:s0 + rows, 0:D_CONV] * wa_ref[k:k + 1, :]
            a = term if a is None else a + term
        ya = cf_ref[0, r0:r0 + rows, 2 * D_CONV:3 * D_CONV] * a
        ybuf[r0:r0 + rows, 0:D_CONV] = ya.astype(BF16)

        acc = None
        for k in range(CONF_W):
            s0 = HALO + r0 + k - CONF_W // 2
            term = cbuf[s0:s0 + rows, D_CONV:2 * D_CONV] * wc_ref[k:k + 1, :]
            acc = term if acc is None else acc + term
        acc = acc + cb_ref[...]
        mu = jnp.mean(acc, axis=-1, keepdims=True)
        xc = acc - mu
        var = jnp.mean(xc * xc, axis=-1, keepdims=True)
        yn = (xc * lax.rsqrt(var + EPS)) * lg_ref[...] + lb_ref[...]
        ybuf[r0:r0 + rows, Y_CONF:D_MODEL] = (yn * _sigmoid(yn)).astype(BF16)

    lane = lax.broadcasted_iota(jnp.int32, (1, LANES), 1)
    low_head = lane < HEAD_DIM
    if local:
        kvbuf[0:WINDOW, :] = kvp_ref[0]
        kvbuf[WINDOW:WINDOW + tb, :] = kv_ref[0]
        kvbuf[WINDOW + tb:WINDOW + tb + WINDOW, :] = kvn_ref[0]
        n_loc = 3 * WINDOW
        qi = lax.broadcasted_iota(jnp.int32, (WINDOW, n_loc), 0)
        si = lax.broadcasted_iota(jnp.int32, (WINDOW, n_loc), 1)
        rel = si - qi

    def sub_block(j, carry):
        j0 = pl.multiple_of(j * WINDOW, WINDOW)
        qj = q_ref[0, pl.ds(j0, WINDOW), :]
        if local:
            base = i * tb + j0
            valid = ((rel >= 0) & (rel <= 2 * WINDOW)
                     & (si >= WINDOW - base) & (si < seq + WINDOW - base))
        for kvh in range(N_KV):
            parts = []
            for g in range(GROUP):
                h = kvh * GROUP + g
                slab = qj[:, (h // 2) * LANES:(h // 2 + 1) * LANES]
                parts.append(jnp.where(low_head if h % 2 == 0 else jnp.logical_not(low_head),
                                       slab, jnp.zeros_like(slab)))
            qs = jnp.concatenate(parts, axis=0)
            kc = ckv_ref[0, :, kvh * LANES:(kvh + 1) * LANES]
            vc = ckv_ref[0, :, (N_KV + kvh) * LANES:(N_KV + kvh + 1) * LANES]
            s_ctx = _dot_nt(qs, kc)
            vcs = (jnp.where(low_head, vc, jnp.zeros_like(vc)),
                   jnp.where(low_head, jnp.zeros_like(vc), vc))
            if local:
                kw = kvbuf[pl.ds(j0, n_loc), kvh * LANES:(kvh + 1) * LANES]
                vw = kvbuf[pl.ds(j0, n_loc), (N_KV + kvh) * LANES:(N_KV + kvh + 1) * LANES]
                s_loc = _dot_nt(qs, kw)
                vws = (jnp.where(low_head, vw, jnp.zeros_like(vw)),
                       jnp.where(low_head, jnp.zeros_like(vw), vw))
            for pair in range(GROUP // 2):
                o_pair = None
                for half in range(2):
                    g = 2 * pair + half
                    sk = sink_ref[kvh * GROUP + g]
                    sc = s_ctx[g * WINDOW:(g + 1) * WINDOW]
                    m = jnp.maximum(jnp.max(sc, axis=-1, keepdims=True), sk)
                    if local:
                        sl = jnp.where(valid, s_loc[g * WINDOW:(g + 1) * WINDOW], NEG_INF)
                        m = jnp.maximum(m, jnp.max(sl, axis=-1, keepdims=True))
                    pc = jnp.exp(sc - m)
                    den = jnp.sum(pc, axis=-1, keepdims=True) + jnp.exp(sk - m)
                    o = _dot(pc.astype(BF16), vcs[half])
                    if local:
                        pw = jnp.exp(sl - m)
                        den = den + jnp.sum(pw, axis=-1, keepdims=True)
                        o = o + _dot(pw.astype(BF16), vws[half])
                    o = o * (1.0 / den)
                    o_pair = o if o_pair is None else o_pair + o
                col = Y_ATTN + (kvh * (GROUP // 2) + pair) * LANES
                ybuf[pl.ds(j0, WINDOW), col:col + LANES] = o_pair.astype(BF16)
        return carry

    lax.fori_loop(0, tb // WINDOW, sub_block, 0)

    out = _dot(ybuf[...], wo_ref[...])
    o_ref[0] = x_ref[0] + g1_ref[0] * out


def _mix(x, cf, q, kv, ckv, mods, mod_row, conv_a_w, conv_c_w, conv_c_b, ln_g, ln_b, sink, w_out_b,
         tb, local):
    b, n, _ = x.shape
    n_ctx = ckv.shape[1]
    const = lambda bb, i: (0, 0)
    hb = tb // HALO
    wb = tb // WINDOW
    kern = functools.partial(_mix_kernel, local, tb, n)
    return pl.pallas_call(
        kern,
        grid=(b, n // tb),
        in_specs=[
            pl.BlockSpec((1, tb, D_MODEL), lambda bb, i: (bb, i, 0)),
            pl.BlockSpec((1, tb, 3 * D_CONV), lambda bb, i: (bb, i, 0)),
            pl.BlockSpec((1, HALO, 3 * D_CONV), lambda bb, i: (bb, jnp.maximum(i * hb - 1, 0), 0)),
            pl.BlockSpec((1, HALO, 3 * D_CONV),
                         lambda bb, i: (bb, jnp.minimum((i + 1) * hb, n // HALO - 1), 0)),
            pl.BlockSpec((1, tb, D_ATTN), lambda bb, i: (bb, i, 0)),
            pl.BlockSpec((1, tb, 4 * LANES), lambda bb, i: (bb, i, 0)),
            pl.BlockSpec((1, WINDOW, 4 * LANES), lambda bb, i: (bb, jnp.maximum(i * wb - 1, 0), 0)),
            pl.BlockSpec((1, WINDOW, 4 * LANES),
                         lambda bb, i: (bb, jnp.minimum((i + 1) * wb, n // WINDOW - 1), 0)),
            pl.BlockSpec((1, n_ctx, 4 * LANES), lambda bb, i: (bb, 0, 0)),
            pl.BlockSpec((SHORT_W, D_CONV), const),
            pl.BlockSpec((CONF_W, D_CONV), const),
            pl.BlockSpec((1, D_CONV), const),
            pl.BlockSpec((1, D_CONV), const),
            pl.BlockSpec((1, D_CONV), const),
            pl.BlockSpec(memory_space=pltpu.SMEM),
            pl.BlockSpec((D_MODEL, D_MODEL), const),
            pl.BlockSpec((1, 1, D_MODEL), lambda bb, i: (mod_row(bb), 0, 2)),
        ],
        out_specs=pl.BlockSpec((1, tb, D_MODEL), lambda bb, i: (bb, i, 0)),
        out_shape=jax.ShapeDtypeStruct((b, n, D_MODEL), F32),
        scratch_shapes=[
            pltpu.VMEM((tb + 2 * HALO, 2 * D_CONV), F32),
            pltpu.VMEM((tb + 2 * WINDOW, 4 * LANES), BF16),
            pltpu.VMEM((tb, D_MODEL), BF16),
        ],
        compiler_params=pltpu.CompilerParams(
            dimension_semantics=("parallel", "parallel"), vmem_limit_bytes=VMEM_LIMIT),
        name="mix_local" if local else "mix_ctx",
    )(x, cf, cf, cf, q, kv, kv, kv, ckv, conv_a_w, conv_c_w, conv_c_b, ln_g, ln_b, sink, w_out_b, mods)


def _mlp_kernel(x_ref, sh_ref, sc_ref, gate_ref, g_ref, w1_ref, w2_ref, o_ref):
    x = x_ref[0]
    ms = jnp.mean(x * x, axis=-1, keepdims=True)
    h = (x * lax.rsqrt(ms + EPS)) * g_ref[...]
    h = h * (1.0 + sc_ref[0]) + sh_ref[0]
    hb = h.astype(BF16)
    acc = None
    for c in range(D_FF // FF_CHUNK):
        hid = _dot(hb, w1_ref[:, c * FF_CHUNK:(c + 1) * FF_CHUNK])
        r = jnp.maximum(hid, 0.0)
        part = _dot((r * r).astype(BF16), w2_ref[c * FF_CHUNK:(c + 1) * FF_CHUNK, :])
        acc = part if acc is None else acc + part
    o_ref[0] = x + gate_ref[0] * acc


def _mlp(x, mods, mod_row, norm_g, w1_b, w2_b, tb):
    b, n, _ = x.shape
    const = lambda bb, i: (0, 0)
    return pl.pallas_call(
        _mlp_kernel,
        grid=(b, n // tb),
        in_specs=[
            pl.BlockSpec((1, tb, D_MODEL), lambda bb, i: (bb, i, 0)),
            pl.BlockSpec((1, 1, D_MODEL), lambda bb, i: (mod_row(bb), 0, 3)),
            pl.BlockSpec((1, 1, D_MODEL), lambda bb, i: (mod_row(bb), 0, 4)),
            pl.BlockSpec((1, 1, D_MODEL), lambda bb, i: (mod_row(bb), 0, 5)),
            pl.BlockSpec((1, D_MODEL), const),
            pl.BlockSpec((D_MODEL, D_FF), const),
            pl.BlockSpec((D_FF, D_MODEL), const),
        ],
        out_specs=pl.BlockSpec((1, tb, D_MODEL), lambda bb, i: (bb, i, 0)),
        out_shape=jax.ShapeDtypeStruct((b, n, D_MODEL), F32),
        compiler_params=pltpu.CompilerParams(
            dimension_semantics=("parallel", "parallel"), vmem_limit_bytes=VMEM_LIMIT),
        name="mlp",
    )(x, mods, mods, mods, norm_g, w1_b, w2_b)


def _rope_tables(seq):
    d_axis = HEAD_DIM // 2
    inv_freq = ROPE_BASE ** (-jnp.arange(0, d_axis, 2, dtype=F32) / d_axis)
    pos = jnp.arange(seq, dtype=jnp.int32)
    ang_r = (pos // GRID_W).astype(F32)[:, None] * inv_freq[None, :]
    ang_c = (pos % GRID_W).astype(F32)[:, None] * inv_freq[None, :]
    cos = jnp.concatenate([jnp.cos(ang_r), jnp.cos(ang_r), jnp.cos(ang_c), jnp.cos(ang_c)], axis=-1)
    sin = jnp.concatenate([-jnp.sin(ang_r), jnp.sin(ang_r), -jnp.sin(ang_c), jnp.sin(ang_c)], axis=-1)
    reps = LANES // HEAD_DIM
    return jnp.tile(cos, (1, reps)), jnp.tile(sin, (1, reps))


def kernel(x, c, ctx, c_ctx, w_mod, b_mod, norm1_g, w_in, conv_a_w, q_norm_g, k_norm_g, attn_sink,
           conv_c_w, conv_c_b, ln_c_g, ln_c_b, w_out, norm2_g, w_mlp1, w_mlp2):
    batch, seq, _ = x.shape
    n_ctx = ctx.shape[1]
    depth = w_mod.shape[0]
    tb = min(TOKEN_BLOCK, seq)
    tb_ctx = n_ctx
    ctx_row = batch

    cvec = jnp.concatenate([c, c_ctx[None, :], jnp.zeros((8 - batch - 1, D_MODEL), F32)], axis=0)
    mods_all = _modulation(cvec, w_mod, b_mod).reshape(depth, 8, 1, 6 * D_MODEL)

    cos_t, sin_t = _rope_tables(seq)
    cos_c = jnp.ones((n_ctx, LANES), F32)
    sin_c = jnp.zeros((n_ctx, LANES), F32)
    reps = LANES // HEAD_DIM
    lat_row = lambda bb: bb
    cx_row = lambda bb: ctx_row

    xc = ctx
    for l in range(depth):
        last = l == depth - 1
        mods = mods_all[l]
        w_in_b = w_in[l].astype(BF16)
        w_out_b = w_out[l].astype(BF16)
        w1_b = w_mlp1[l].astype(BF16)
        w2_b = w_mlp2[l].astype(BF16)
        g1n = norm1_g[l][None, :]
        g2n = norm2_g[l][None, :]
        gq = jnp.tile(q_norm_g[l], reps)[None, :]
        gk = jnp.tile(k_norm_g[l], reps)[None, :]
        conv_args = (conv_a_w[l], conv_c_w[l], conv_c_b[l][None, :], ln_c_g[l][None, :],
                     ln_c_b[l][None, :], attn_sink[l], w_out_b)

        cf_c, q_c, kv_c = _proj(xc, mods, cx_row, g1n, w_in_b, cos_c, sin_c, gq, gk, tb_ctx)
        cf, q, kv = _proj(x, mods, lat_row, g1n, w_in_b, cos_t, sin_t, gq, gk, tb)
        x = _mix(x, cf, q, kv, kv_c, mods, lat_row, *conv_args, tb=tb, local=True)
        x = _mlp(x, mods, lat_row, g2n, w1_b, w2_b, tb)
        if not last:
            xc = _mix(xc, cf_c, q_c, kv_c, kv_c, mods, cx_row, *conv_args, tb=tb_ctx, local=False)
            xc = _mlp(xc, mods, cx_row, g2n, w1_b, w2_b, tb_ctx)
    return x
```

```python
import functools

import jax
import jax.numpy as jnp
from jax import lax
from jax.experimental import pallas as pl
from jax.experimental.pallas import tpu as pltpu

F32 = jnp.float32
BF16 = jnp.bfloat16

D_MODEL = 1024
HEAD_DIM = 64
N_HEADS = 8
N_KV = 2
GROUP = N_HEADS // N_KV
D_CONV = 256
D_ATTN = N_HEADS * HEAD_DIM
D_FF = 4 * D_MODEL
GRID_W = 64
WINDOW = 128
SHORT_W = 3
CONF_W = 31
HALO = 16
ROPE_BASE = 10000.0
EPS = 1e-6
NEG_INF = -1e30
LOG2E = 1.4426950408889634

OFF_Q = 3 * D_CONV
OFF_K = OFF_Q + D_ATTN
OFF_C = OFF_K + 2 * N_KV * HEAD_DIM
D_IN = OFF_C + 2 * D_CONV
Y_ATTN = D_CONV
Y_CONF = D_CONV + D_ATTN

LANES = 128
TOKEN_BLOCK = 512
FF_CHUNK = 1024
MOD_COLS = 1536
VMEM_LIMIT = 56 * 1024 * 1024


def _sigmoid(t):
    return 1.0 / (1.0 + jnp.exp(-t))


def _dot(a, b):
    return jnp.dot(a, b, preferred_element_type=F32)


def _dot_nt(a, b):
    return lax.dot_general(a, b, (((1,), (1,)), ((), ())), preferred_element_type=F32)


def _mod_kernel(c_ref, w_ref, b_ref, o_ref):
    c = c_ref[...]
    s = c * _sigmoid(c)
    o_ref[0] = jnp.dot(s, w_ref[0], preferred_element_type=F32,
                       precision=lax.Precision.HIGHEST) + b_ref[0]


def _modulation(cvec, w_mod, b_mod):
    depth, _, n = w_mod.shape
    return pl.pallas_call(
        _mod_kernel,
        grid=(depth, n // MOD_COLS),
        in_specs=[
            pl.BlockSpec((8, D_MODEL), lambda l, j: (0, 0)),
            pl.BlockSpec((1, D_MODEL, MOD_COLS), lambda l, j: (l, 0, j)),
            pl.BlockSpec((1, 1, MOD_COLS), lambda l, j: (l, 0, j)),
        ],
        out_specs=pl.BlockSpec((1, 8, MOD_COLS), lambda l, j: (l, 0, j)),
        out_shape=jax.ShapeDtypeStruct((depth, 8, n), F32),
        compiler_params=pltpu.CompilerParams(
            dimension_semantics=("arbitrary", "arbitrary"), vmem_limit_bytes=VMEM_LIMIT),
        name="modulation",
    )(cvec, w_mod, b_mod.reshape(depth, 1, n))


def _proj_kernel(x_ref, sh_ref, sc_ref, g_ref, w_ref, cos_ref, sin_ref, gq_ref, gk_ref,
                 cf_ref, q_ref, kv_ref):
    x = x_ref[0]
    ms = jnp.mean(x * x, axis=-1, keepdims=True)
    h = (x * lax.rsqrt(ms + EPS)) * g_ref[...]
    h = h * (1.0 + sc_ref[0]) + sh_ref[0]
    hb = h.astype(BF16)

    ua = _dot(hb, w_ref[:, 0:OFF_Q])
    cf_ref[0, :, 0:D_CONV] = ua[:, 2 * D_CONV:3 * D_CONV] * ua[:, 0:D_CONV]
    cf_ref[0, :, 2 * D_CONV:3 * D_CONV] = ua[:, D_CONV:2 * D_CONV]

    uc = _dot(hb, w_ref[:, OFF_C:D_IN])
    cf_ref[0, :, D_CONV:2 * D_CONV] = uc[:, 0:D_CONV] * _sigmoid(uc[:, D_CONV:2 * D_CONV])

    lane = lax.broadcasted_iota(jnp.int32, (1, LANES), 1)
    rot_from_above = (lane % (HEAD_DIM // 2)) < (HEAD_DIM // 4)
    low_head = lane < HEAD_DIM
    r_i = lax.broadcasted_iota(jnp.int32, (LANES, LANES), 0) // HEAD_DIM
    c_i = lax.broadcasted_iota(jnp.int32, (LANES, LANES), 1) // HEAD_DIM
    head_ones = jnp.where(r_i == c_i, 1.0, 0.0).astype(BF16)
    cos = cos_ref[...]
    sin = sin_ref[...]

    def norm_rope(t, gain, scale):
        ssq = _dot((t * t).astype(BF16), head_ones)
        rs = lax.rsqrt(ssq * (1.0 / HEAD_DIM) + EPS) * scale
        tg = t * gain
        partner = jnp.where(rot_from_above, pltpu.roll(tg, LANES - HEAD_DIM // 4, 1),
                            pltpu.roll(tg, HEAD_DIM // 4, 1))
        return rs * (tg * cos + partner * sin)

    uq = _dot(hb, w_ref[:, OFF_Q:OFF_K])
    gq = gq_ref[...]
    for s in range(D_ATTN // LANES):
        qs = norm_rope(uq[:, s * LANES:(s + 1) * LANES], gq, LOG2E * HEAD_DIM ** -0.5)
        q_ref[0, :, s * LANES:(s + 1) * LANES] = qs.astype(BF16)

    ukv = _dot(hb, w_ref[:, OFF_K:OFF_C])
    kn = norm_rope(ukv[:, 0:LANES], gk_ref[...], 1.0)
    for j, t in enumerate((kn, ukv[:, LANES:2 * LANES])):
        swapped = pltpu.roll(t, HEAD_DIM, 1)
        kv_ref[0, :, (2 * j) * LANES:(2 * j + 1) * LANES] = jnp.where(low_head, t, swapped).astype(BF16)
        kv_ref[0, :, (2 * j + 1) * LANES:(2 * j + 2) * LANES] = jnp.where(low_head, swapped, t).astype(BF16)


def _proj(x, mods, mod_row, norm_g, w_in_b, cos_t, sin_t, gq, gk, tb):
    b, n, _ = x.shape
    const = lambda bb, i: (0, 0)
    return pl.pallas_call(
        _proj_kernel,
        grid=(b, n // tb),
        in_specs=[
            pl.BlockSpec((1, tb, D_MODEL), lambda bb, i: (bb, i, 0)),
            pl.BlockSpec((1, 1, D_MODEL), lambda bb, i: (mod_row(bb), 0, 0)),
            pl.BlockSpec((1, 1, D_MODEL), lambda bb, i: (mod_row(bb), 0, 1)),
            pl.BlockSpec((1, D_MODEL), const),
            pl.BlockSpec((D_MODEL, D_IN), const),
            pl.BlockSpec((tb, LANES), lambda bb, i: (i, 0)),
            pl.BlockSpec((tb, LANES), lambda bb, i: (i, 0)),
            pl.BlockSpec((1, LANES), const),
            pl.BlockSpec((1, LANES), const),
        ],
        out_specs=[
            pl.BlockSpec((1, tb, 3 * D_CONV), lambda bb, i: (bb, i, 0)),
            pl.BlockSpec((1, tb, D_ATTN), lambda bb, i: (bb, i, 0)),
            pl.BlockSpec((1, tb, 4 * LANES), lambda bb, i: (bb, i, 0)),
        ],
        out_shape=[
            jax.ShapeDtypeStruct((b, n, 3 * D_CONV), F32),
            jax.ShapeDtypeStruct((b, n, D_ATTN), BF16),
            jax.ShapeDtypeStruct((b, n, 4 * LANES), BF16),
        ],
        compiler_params=pltpu.CompilerParams(
            dimension_semantics=("parallel", "parallel"), vmem_limit_bytes=VMEM_LIMIT),
        name="proj",
    )(x, mods, mods, norm_g, w_in_b, cos_t, sin_t, gq, gk)


def _phase_conv(buf, col0, w_ref, n_taps, r0, rows):
    phases = {}
    for k in range(n_taps):
        off = HALO + k - n_taps // 2
        phases.setdefault(off % 8, []).append((off - off % 8, k))
    out = None
    for p in sorted(phases):
        n_rows = rows if p == 0 else rows + 8
        z = None
        for a8, k in phases[p]:
            start = pl.multiple_of(r0 + a8, 8)
            term = buf[pl.ds(start, n_rows), col0:col0 + D_CONV] * w_ref[k:k + 1, :]
            z = term if z is None else z + term
        z = z[p:p + rows]
        out = z if out is None else out + z
    return out


def _mix_kernel(local, tb, seq,
                x_ref, cf_ref, cfp_ref, cfn_ref, q_ref, kv_ref, kvp_ref, kvn_ref, ckv_ref,
                wa_ref, wc_ref, cb_ref, lg_ref, lb_ref, sink_ref, wo_ref, g1_ref,
                o_ref, cbuf, kvbuf, ybuf):
    i = pl.program_id(1)
    has_prev = i > 0
    has_next = i < pl.num_programs(1) - 1

    cbuf[0:HALO, :] = jnp.where(has_prev, cfp_ref[0, :, 0:2 * D_CONV], 0.0)
    cbuf[HALO:HALO + tb, :] = cf_ref[0, :, 0:2 * D_CONV]
    cbuf[HALO + tb:HALO + tb + HALO, :] = jnp.where(has_next, cfn_ref[0, :, 0:2 * D_CONV], 0.0)

    lane = lax.broadcasted_iota(jnp.int32, (1, LANES), 1)
    low_head = lane < HEAD_DIM
    if local:
        kvbuf[0:WINDOW, :] = kvp_ref[0]
        kvbuf[WINDOW:WINDOW + tb, :] = kv_ref[0]
        kvbuf[WINDOW + tb:WINDOW + tb + WINDOW, :] = kvn_ref[0]
        n_loc = 3 * WINDOW
        rel = (lax.broadcasted_iota(jnp.int32, (WINDOW, WINDOW), 1)
               - lax.broadcasted_iota(jnp.int32, (WINDOW, WINDOW), 0))
    ones_cols = jnp.ones((1, LANES), BF16)

    def with_ones(v, keep_low):
        zero = jnp.zeros_like(v)
        masked = jnp.where(low_head, v, zero) if keep_low else jnp.where(low_head, zero, v)
        return jnp.concatenate([masked, jnp.broadcast_to(ones_cols, v.shape)], axis=1)

    def mixer_rows(j, prev=None):
        j0 = pl.multiple_of(j * WINDOW, WINDOW)

        half_rows = WINDOW // 2
        y_short, y_conf = [], []
        for c in range(2):
            r0 = j0 + c * half_rows
            a = _phase_conv(cbuf, 0, wa_ref, SHORT_W, r0, half_rows)
            gate = cf_ref[0, pl.ds(pl.multiple_of(r0, half_rows), half_rows), 2 * D_CONV:3 * D_CONV]
            y_short.append((gate * a).astype(BF16))
            acc = _phase_conv(cbuf, D_CONV, wc_ref, CONF_W, r0, half_rows) + cb_ref[...]
            mu = jnp.mean(acc, axis=-1, keepdims=True)
            xc = acc - mu
            var = jnp.mean(xc * xc, axis=-1, keepdims=True)
            yn = (xc * lax.rsqrt(var + EPS)) * lg_ref[...] + lb_ref[...]
            y_conf.append((yn * _sigmoid(yn)).astype(BF16))
        y_parts = [jnp.concatenate(y_short, axis=0)]

        qj = q_ref[0, pl.ds(j0, WINDOW), :]
        if local:
            base = i * tb + j0
            big = jnp.int32(1 << 20)
            prev_ok = rel >= jnp.where(base >= WINDOW, 0, big)
            next_ok = rel <= jnp.where(base + 2 * WINDOW <= seq, 0, -big)
        scores = []
        for kvh in range(N_KV):
            parts = []
            for g in range(GROUP):
                h = kvh * GROUP + g
                slab = qj[:, (h // 2) * LANES:(h // 2 + 1) * LANES]
                parts.append(jnp.where(low_head if h % 2 == 0 else jnp.logical_not(low_head),
                                       slab, jnp.zeros_like(slab)))
            qs = jnp.concatenate(parts, axis=0)
            kc = ckv_ref[0, :, kvh * LANES:(kvh + 1) * LANES]
            vc = ckv_ref[0, :, (N_KV + kvh) * LANES:(N_KV + kvh + 1) * LANES]
            s_ctx = _dot_nt(qs, kc)
            vcs = (with_ones(vc, True), with_ones(vc, False))
            s_loc = vws = None
            if local:
                kw = kvbuf[pl.ds(j0, n_loc), kvh * LANES:(kvh + 1) * LANES]
                vw = kvbuf[pl.ds(j0, n_loc), (N_KV + kvh) * LANES:(N_KV + kvh + 1) * LANES]
                s_loc = _dot_nt(qs, kw)
                vws = (with_ones(vw, True), with_ones(vw, False))
            scores.append((s_ctx, vcs, s_loc, vws))
            if kvh == 0 and prev is not None:
                project(prev)
        for kvh in range(N_KV):
            s_ctx, vcs, s_loc, vws = scores[kvh]
            for pair in range(GROUP // 2):
                num, den = [], []
                for half in range(2):
                    g = 2 * pair + half
                    sk = sink_ref[kvh * GROUP + g] * LOG2E
                    sc = s_ctx[g * WINDOW:(g + 1) * WINDOW]
                    if local:
                        sg = s_loc[g * WINDOW:(g + 1) * WINDOW]
                        sl = jnp.concatenate([
                            jnp.where(prev_ok, sg[:, 0:WINDOW], NEG_INF),
                            sg[:, WINDOW:2 * WINDOW],
                            jnp.where(next_ok, sg[:, 2 * WINDOW:3 * WINDOW], NEG_INF)], axis=1)
                        m = jnp.maximum(jnp.maximum(jnp.max(sl, axis=-1, keepdims=True),
                                                    jnp.max(sc, axis=-1, keepdims=True)), sk)
                        o = (_dot(jnp.exp2(sl - m).astype(BF16), vws[half])
                             + _dot(jnp.exp2(sc - m).astype(BF16), vcs[half]))
                    else:
                        m = jnp.maximum(jnp.max(sc, axis=-1, keepdims=True), sk)
                        o = _dot(jnp.exp2(sc - m).astype(BF16), vcs[half])
                    num.append(o[:, 0:LANES])
                    den.append(o[:, LANES:2 * LANES] + jnp.exp2(sk - m))
                o_pair = (num[0] + num[1]) / jnp.where(low_head, den[0], den[1])
                y_parts.append(o_pair.astype(BF16))
        y_parts.append(jnp.concatenate(y_conf, axis=0))

        return jnp.concatenate(y_parts, axis=1)

    def project(j):
        j0 = pl.multiple_of(j * WINDOW, WINDOW)
        out = _dot(ybuf[pl.ds(j0, WINDOW), :], wo_ref[...])
        o_ref[0, pl.ds(j0, WINDOW), :] = x_ref[0, pl.ds(j0, WINDOW), :] + g1_ref[0] * out

    n_blocks = tb // WINDOW
    ybuf[0:WINDOW, :] = mixer_rows(0)

    def step(j, carry):
        ybuf[pl.ds(pl.multiple_of(j * WINDOW, WINDOW), WINDOW), :] = mixer_rows(j, j - 1)
        return carry

    lax.fori_loop(1, n_blocks, step, 0)
    project(n_blocks - 1)


def _mix(x, cf, q, kv, ckv, mods, mod_row, conv_a_w, conv_c_w, conv_c_b, ln_g, ln_b, sink, w_out_b,
         tb, local):
    b, n, _ = x.shape
    n_ctx = ckv.shape[1]
    const = lambda bb, i: (0, 0)
    hb = tb // HALO
    wb = tb // WINDOW
    kern = functools.partial(_mix_kernel, local, tb, n)
    return pl.pallas_call(
        kern,
        grid=(b, n // tb),
        in_specs=[
            pl.BlockSpec((1, tb, D_MODEL), lambda bb, i: (bb, i, 0)),
            pl.BlockSpec((1, tb, 3 * D_CONV), lambda bb, i: (bb, i, 0)),
            pl.BlockSpec((1, HALO, 3 * D_CONV), lambda bb, i: (bb, jnp.maximum(i * hb - 1, 0), 0)),
            pl.BlockSpec((1, HALO, 3 * D_CONV),
                         lambda bb, i: (bb, jnp.minimum((i + 1) * hb, n // HALO - 1), 0)),
            pl.BlockSpec((1, tb, D_ATTN), lambda bb, i: (bb, i, 0)),
            pl.BlockSpec((1, tb, 4 * LANES), lambda bb, i: (bb, i, 0)),
            pl.BlockSpec((1, WINDOW, 4 * LANES), lambda bb, i: (bb, jnp.maximum(i * wb - 1, 0), 0)),
            pl.BlockSpec((1, WINDOW, 4 * LANES),
                         lambda bb, i: (bb, jnp.minimum((i + 1) * wb, n // WINDOW - 1), 0)),
            pl.BlockSpec((1, n_ctx, 4 * LANES), lambda bb, i: (bb, 0, 0)),
            pl.BlockSpec((SHORT_W, D_CONV), const),
            pl.BlockSpec((CONF_W, D_CONV), const),
            pl.BlockSpec((1, D_CONV), const),
            pl.BlockSpec((1, D_CONV), const),
            pl.BlockSpec((1, D_CONV), const),
            pl.BlockSpec(memory_space=pltpu.SMEM),
            pl.BlockSpec((D_MODEL, D_MODEL), const),
            pl.BlockSpec((1, 1, D_MODEL), lambda bb, i: (mod_row(bb), 0, 2)),
        ],
        out_specs=pl.BlockSpec((1, tb, D_MODEL), lambda bb, i: (bb, i, 0)),
        out_shape=jax.ShapeDtypeStruct((b, n, D_MODEL), F32),
        scratch_shapes=[
            pltpu.VMEM((tb + 2 * HALO, 2 * D_CONV), F32),
            pltpu.VMEM((tb + 2 * WINDOW, 4 * LANES), BF16),
            pltpu.VMEM((tb, D_MODEL), BF16),
        ],
        compiler_params=pltpu.CompilerParams(
            dimension_semantics=("parallel", "parallel"), vmem_limit_bytes=VMEM_LIMIT),
        name="mix_local" if local else "mix_ctx",
    )(x, cf, cf, cf, q, kv, kv, kv, ckv, conv_a_w, conv_c_w, conv_c_b, ln_g, ln_b, sink, w_out_b, mods)


def _mlp_kernel(x_ref, sh_ref, sc_ref, gate_ref, g_ref, w1_ref, w2_ref, o_ref):
    x = x_ref[0]
    ms = jnp.mean(x * x, axis=-1, keepdims=True)
    h = (x * lax.rsqrt(ms + EPS)) * g_ref[...]
    h = h * (1.0 + sc_ref[0]) + sh_ref[0]
    hb = h.astype(BF16)
    acc = None
    for c in range(D_FF // FF_CHUNK):
        hid = _dot(hb, w1_ref[:, c * FF_CHUNK:(c + 1) * FF_CHUNK])
        r = jnp.maximum(hid, 0.0)
        part = _dot((r * r).astype(BF16), w2_ref[c * FF_CHUNK:(c + 1) * FF_CHUNK, :])
        acc = part if acc is None else acc + part
    o_ref[0] = x + gate_ref[0] * acc


def _mlp(x, mods, mod_row, norm_g, w1_b, w2_b, tb):
    b, n, _ = x.shape
    const = lambda bb, i: (0, 0)
    return pl.pallas_call(
        _mlp_kernel,
        grid=(b, n // tb),
        in_specs=[
            pl.BlockSpec((1, tb, D_MODEL), lambda bb, i: (bb, i, 0)),
            pl.BlockSpec((1, 1, D_MODEL), lambda bb, i: (mod_row(bb), 0, 3)),
            pl.BlockSpec((1, 1, D_MODEL), lambda bb, i: (mod_row(bb), 0, 4)),
            pl.BlockSpec((1, 1, D_MODEL), lambda bb, i: (mod_row(bb), 0, 5)),
            pl.BlockSpec((1, D_MODEL), const),
            pl.BlockSpec((D_MODEL, D_FF), const),
            pl.BlockSpec((D_FF, D_MODEL), const),
        ],
        out_specs=pl.BlockSpec((1, tb, D_MODEL), lambda bb, i: (bb, i, 0)),
        out_shape=jax.ShapeDtypeStruct((b, n, D_MODEL), F32),
        compiler_params=pltpu.CompilerParams(
            dimension_semantics=("parallel", "parallel"), vmem_limit_bytes=VMEM_LIMIT),
        name="mlp",
    )(x, mods, mods, mods, norm_g, w1_b, w2_b)


def _rope_tables(seq):
    d_axis = HEAD_DIM // 2
    inv_freq = ROPE_BASE ** (-jnp.arange(0, d_axis, 2, dtype=F32) / d_axis)
    pos = jnp.arange(seq, dtype=jnp.int32)
    ang_r = (pos // GRID_W).astype(F32)[:, None] * inv_freq[None, :]
    ang_c = (pos % GRID_W).astype(F32)[:, None] * inv_freq[None, :]
    cos = jnp.concatenate([jnp.cos(ang_r), jnp.cos(ang_r), jnp.cos(ang_c), jnp.cos(ang_c)], axis=-1)
    sin = jnp.concatenate([-jnp.sin(ang_r), jnp.sin(ang_r), -jnp.sin(ang_c), jnp.sin(ang_c)], axis=-1)
    reps = LANES // HEAD_DIM
    return jnp.tile(cos, (1, reps)), jnp.tile(sin, (1, reps))


def kernel(x, c, ctx, c_ctx, w_mod, b_mod, norm1_g, w_in, conv_a_w, q_norm_g, k_norm_g, attn_sink,
           conv_c_w, conv_c_b, ln_c_g, ln_c_b, w_out, norm2_g, w_mlp1, w_mlp2):
    batch, seq, _ = x.shape
    n_ctx = ctx.shape[1]
    depth = w_mod.shape[0]
    tb = min(TOKEN_BLOCK, seq)
    tb_ctx = n_ctx
    ctx_row = batch

    cvec = jnp.concatenate([c, c_ctx[None, :], jnp.zeros((8 - batch - 1, D_MODEL), F32)], axis=0)
    mods_all = _modulation(cvec, w_mod, b_mod).reshape(depth, 8, 1, 6 * D_MODEL)

    cos_t, sin_t = _rope_tables(seq)
    cos_c = jnp.ones((n_ctx, LANES), F32)
    sin_c = jnp.zeros((n_ctx, LANES), F32)
    reps = LANES // HEAD_DIM
    lat_row = lambda bb: bb
    cx_row = lambda bb: ctx_row

    xc = ctx
    for l in range(depth):
        last = l == depth - 1
        mods = mods_all[l]
        w_in_b = w_in[l].astype(BF16)
        w_out_b = w_out[l].astype(BF16)
        w1_b = w_mlp1[l].astype(BF16)
        w2_b = w_mlp2[l].astype(BF16)
        g1n = norm1_g[l][None, :]
        g2n = norm2_g[l][None, :]
        gq = jnp.tile(q_norm_g[l], reps)[None, :]
        gk = jnp.tile(k_norm_g[l], reps)[None, :]
        conv_args = (conv_a_w[l], conv_c_w[l], conv_c_b[l][None, :], ln_c_g[l][None, :],
                     ln_c_b[l][None, :], attn_sink[l], w_out_b)

        cf_c, q_c, kv_c = _proj(xc, mods, cx_row, g1n, w_in_b, cos_c, sin_c, gq, gk, tb_ctx)
        cf, q, kv = _proj(x, mods, lat_row, g1n, w_in_b, cos_t, sin_t, gq, gk, tb)
        x = _mix(x, cf, q, kv, kv_c, mods, lat_row, *conv_args, tb=tb, local=True)
        x = _mlp(x, mods, lat_row, g2n, w1_b, w2_b, tb)
        if not last:
            xc = _mix(xc, cf_c, q_c, kv_c, kv_c, mods, cx_row, *conv_args, tb=tb_ctx, local=False)
            xc = _mlp(xc, mods, cx_row, g2n, w1_b, w2_b, tb_ctx)
    return x
```

```python
import functools

import jax
import jax.numpy as jnp
from jax import lax
from jax.experimental import pallas as pl
from jax.experimental.pallas import tpu as pltpu

F32 = jnp.float32
BF16 = jnp.bfloat16

D_MODEL = 1024
HEAD_DIM = 64
N_HEADS = 8
N_KV = 2
GROUP = N_HEADS // N_KV
D_CONV = 256
D_ATTN = N_HEADS * HEAD_DIM
D_FF = 4 * D_MODEL
GRID_W = 64
WINDOW = 128
SHORT_W = 3
CONF_W = 31
HALO = 16
ROPE_BASE = 10000.0
EPS = 1e-6
NEG_INF = -1e30
LOG2E = 1.4426950408889634

OFF_Q = 3 * D_CONV
OFF_K = OFF_Q + D_ATTN
OFF_C = OFF_K + 2 * N_KV * HEAD_DIM
D_IN = OFF_C + 2 * D_CONV
Y_ATTN = D_CONV
Y_CONF = D_CONV + D_ATTN

LANES = 128
TOKEN_BLOCK = 512
FF_CHUNK = 1024
MLP_COLS = 256
MOD_COLS = 1536
VMEM_LIMIT = 56 * 1024 * 1024


def _sigmoid(t):
    return 1.0 / (1.0 + jnp.exp(-t))


def _dot(a, b):
    return jnp.dot(a, b, preferred_element_type=F32)


def _dot_nt(a, b):
    return lax.dot_general(a, b, (((1,), (1,)), ((), ())), preferred_element_type=F32)


def _mod_kernel(c_ref, w_ref, b_ref, o_ref):
    c = c_ref[...]
    s = c * _sigmoid(c)
    o_ref[0] = jnp.dot(s, w_ref[0], preferred_element_type=F32,
                       precision=lax.Precision.HIGHEST) + b_ref[0]


def _modulation(cvec, w_mod, b_mod):
    depth, _, n = w_mod.shape
    return pl.pallas_call(
        _mod_kernel,
        grid=(depth, n // MOD_COLS),
        in_specs=[
            pl.BlockSpec((8, D_MODEL), lambda l, j: (0, 0)),
            pl.BlockSpec((1, D_MODEL, MOD_COLS), lambda l, j: (l, 0, j)),
            pl.BlockSpec((1, 1, MOD_COLS), lambda l, j: (l, 0, j)),
        ],
        out_specs=pl.BlockSpec((1, 8, MOD_COLS), lambda l, j: (l, 0, j)),
        out_shape=jax.ShapeDtypeStruct((depth, 8, n), F32),
        compiler_params=pltpu.CompilerParams(
            dimension_semantics=("arbitrary", "arbitrary"), vmem_limit_bytes=VMEM_LIMIT),
        name="modulation",
    )(cvec, w_mod, b_mod.reshape(depth, 1, n))


def _proj_kernel(x_ref, sh_ref, sc_ref, g_ref, w_ref, cos_ref, sin_ref, gq_ref, gk_ref,
                 cf_ref, q_ref, kv_ref):
    x = x_ref[0]
    ms = jnp.mean(x * x, axis=-1, keepdims=True)
    h = (x * lax.rsqrt(ms + EPS)) * g_ref[...]
    h = h * (1.0 + sc_ref[0]) + sh_ref[0]
    hb = h.astype(BF16)

    ua = _dot(hb, w_ref[:, 0:OFF_Q])
    cf_ref[0, :, 0:D_CONV] = ua[:, 2 * D_CONV:3 * D_CONV] * ua[:, 0:D_CONV]
    cf_ref[0, :, 2 * D_CONV:3 * D_CONV] = ua[:, D_CONV:2 * D_CONV]

    uc = _dot(hb, w_ref[:, OFF_C:D_IN])
    cf_ref[0, :, D_CONV:2 * D_CONV] = uc[:, 0:D_CONV] * _sigmoid(uc[:, D_CONV:2 * D_CONV])

    lane = lax.broadcasted_iota(jnp.int32, (1, LANES), 1)
    rot_from_above = (lane % (HEAD_DIM // 2)) < (HEAD_DIM // 4)
    low_head = lane < HEAD_DIM
    r_i = lax.broadcasted_iota(jnp.int32, (LANES, LANES), 0) // HEAD_DIM
    c_i = lax.broadcasted_iota(jnp.int32, (LANES, LANES), 1) // HEAD_DIM
    head_ones = jnp.where(r_i == c_i, 1.0, 0.0).astype(BF16)
    cos = cos_ref[...]
    sin = sin_ref[...]

    def norm_rope(t, gain, scale):
        ssq = _dot((t * t).astype(BF16), head_ones)
        rs = lax.rsqrt(ssq * (1.0 / HEAD_DIM) + EPS) * scale
        tg = t * gain
        partner = jnp.where(rot_from_above, pltpu.roll(tg, LANES - HEAD_DIM // 4, 1),
                            pltpu.roll(tg, HEAD_DIM // 4, 1))
        return rs * (tg * cos + partner * sin)

    uq = _dot(hb, w_ref[:, OFF_Q:OFF_K])
    gq = gq_ref[...]
    for s in range(D_ATTN // LANES):
        qs = norm_rope(uq[:, s * LANES:(s + 1) * LANES], gq, LOG2E * HEAD_DIM ** -0.5)
        q_ref[0, :, s * LANES:(s + 1) * LANES] = qs.astype(BF16)

    ukv = _dot(hb, w_ref[:, OFF_K:OFF_C])
    kn = norm_rope(ukv[:, 0:LANES], gk_ref[...], 1.0)
    for j, t in enumerate((kn, ukv[:, LANES:2 * LANES])):
        swapped = pltpu.roll(t, HEAD_DIM, 1)
        kv_ref[0, :, (2 * j) * LANES:(2 * j + 1) * LANES] = jnp.where(low_head, t, swapped).astype(BF16)
        kv_ref[0, :, (2 * j + 1) * LANES:(2 * j + 2) * LANES] = jnp.where(low_head, swapped, t).astype(BF16)


def _proj(x, mods, mod_row, norm_g, w_in_b, cos_t, sin_t, gq, gk, tb):
    b, n, _ = x.shape
    const = lambda bb, i: (0, 0)
    return pl.pallas_call(
        _proj_kernel,
        grid=(b, n // tb),
        in_specs=[
            pl.BlockSpec((1, tb, D_MODEL), lambda bb, i: (bb, i, 0)),
            pl.BlockSpec((1, 1, D_MODEL), lambda bb, i: (mod_row(bb), 0, 0)),
            pl.BlockSpec((1, 1, D_MODEL), lambda bb, i: (mod_row(bb), 0, 1)),
            pl.BlockSpec((1, D_MODEL), const),
            pl.BlockSpec((D_MODEL, D_IN), const),
            pl.BlockSpec((tb, LANES), lambda bb, i: (i, 0)),
            pl.BlockSpec((tb, LANES), lambda bb, i: (i, 0)),
            pl.BlockSpec((1, LANES), const),
            pl.BlockSpec((1, LANES), const),
        ],
        out_specs=[
            pl.BlockSpec((1, tb, 3 * D_CONV), lambda bb, i: (bb, i, 0)),
            pl.BlockSpec((1, tb, D_ATTN), lambda bb, i: (bb, i, 0)),
            pl.BlockSpec((1, tb, 4 * LANES), lambda bb, i: (bb, i, 0)),
        ],
        out_shape=[
            jax.ShapeDtypeStruct((b, n, 3 * D_CONV), F32),
            jax.ShapeDtypeStruct((b, n, D_ATTN), BF16),
            jax.ShapeDtypeStruct((b, n, 4 * LANES), BF16),
        ],
        compiler_params=pltpu.CompilerParams(
            dimension_semantics=("parallel", "parallel"), vmem_limit_bytes=VMEM_LIMIT),
        name="proj",
    )(x, mods, mods, norm_g, w_in_b, cos_t, sin_t, gq, gk)


def _phase_conv(buf, col0, w_ref, n_taps, r0, rows):
    phases = {}
    for k in range(n_taps):
        off = HALO + k - n_taps // 2
        phases.setdefault(off % 8, []).append((off - off % 8, k))
    out = None
    for p in sorted(phases):
        n_rows = rows if p == 0 else rows + 8
        z = None
        for a8, k in phases[p]:
            start = pl.multiple_of(r0 + a8, 8)
            term = buf[pl.ds(start, n_rows), col0:col0 + D_CONV] * w_ref[k:k + 1, :]
            z = term if z is None else z + term
        z = z[p:p + rows]
        out = z if out is None else out + z
    return out


def _mixer_setup(local, tb, seq, i, cf_ref, cfp_ref, cfn_ref, q_ref, kv_ref, kvp_ref, kvn_ref, ckv_ref,
                 wa_ref, wc_ref, cb_ref, lg_ref, lb_ref, sink_ref, cbuf, kvbuf):
    has_prev = i > 0
    has_next = i < seq // tb - 1

    cbuf[0:HALO, :] = jnp.where(has_prev, cfp_ref[0, :, 0:2 * D_CONV], 0.0)
    cbuf[HALO:HALO + tb, :] = cf_ref[0, :, 0:2 * D_CONV]
    cbuf[HALO + tb:HALO + tb + HALO, :] = jnp.where(has_next, cfn_ref[0, :, 0:2 * D_CONV], 0.0)

    lane = lax.broadcasted_iota(jnp.int32, (1, LANES), 1)
    low_head = lane < HEAD_DIM
    if local:
        kvbuf[0:WINDOW, :] = kvp_ref[0]
        kvbuf[WINDOW:WINDOW + tb, :] = kv_ref[0]
        kvbuf[WINDOW + tb:WINDOW + tb + WINDOW, :] = kvn_ref[0]
        n_loc = 3 * WINDOW
        rel = (lax.broadcasted_iota(jnp.int32, (WINDOW, WINDOW), 1)
               - lax.broadcasted_iota(jnp.int32, (WINDOW, WINDOW), 0))
    ones_cols = jnp.ones((1, LANES), BF16)

    def with_ones(v, keep_low):
        zero = jnp.zeros_like(v)
        masked = jnp.where(low_head, v, zero) if keep_low else jnp.where(low_head, zero, v)
        return jnp.concatenate([masked, jnp.broadcast_to(ones_cols, v.shape)], axis=1)

    def mixer_rows(j, fillers=()):
        fillers = list(fillers)

        def fill():
            if fillers:
                f = fillers.pop(0)
                if f is not None:
                    f()

        j0 = pl.multiple_of(j * WINDOW, WINDOW)

        half_rows = WINDOW // 2
        y_short, y_conf = [], []
        for c in range(2):
            r0 = j0 + c * half_rows
            a = _phase_conv(cbuf, 0, wa_ref, SHORT_W, r0, half_rows)
            gate = cf_ref[0, pl.ds(pl.multiple_of(r0, half_rows), half_rows), 2 * D_CONV:3 * D_CONV]
            y_short.append((gate * a).astype(BF16))
            acc = _phase_conv(cbuf, D_CONV, wc_ref, CONF_W, r0, half_rows) + cb_ref[...]
            mu = jnp.mean(acc, axis=-1, keepdims=True)
            xc = acc - mu
            var = jnp.mean(xc * xc, axis=-1, keepdims=True)
            yn = (xc * lax.rsqrt(var + EPS)) * lg_ref[...] + lb_ref[...]
            y_conf.append((yn * _sigmoid(yn)).astype(BF16))
        y_parts = [jnp.concatenate(y_short, axis=0)]

        qj = q_ref[0, pl.ds(j0, WINDOW), :]
        if local:
            base = i * tb + j0
            big = jnp.int32(1 << 20)
            prev_ok = rel >= jnp.where(base >= WINDOW, 0, big)
            next_ok = rel <= jnp.where(base + 2 * WINDOW <= seq, 0, -big)
        scores = []
        for kvh in range(N_KV):
            parts = []
            for g in range(GROUP):
                h = kvh * GROUP + g
                slab = qj[:, (h // 2) * LANES:(h // 2 + 1) * LANES]
                parts.append(jnp.where(low_head if h % 2 == 0 else jnp.logical_not(low_head),
                                       slab, jnp.zeros_like(slab)))
            qs = jnp.concatenate(parts, axis=0)
            kc = ckv_ref[0, :, kvh * LANES:(kvh + 1) * LANES]
            vc = ckv_ref[0, :, (N_KV + kvh) * LANES:(N_KV + kvh + 1) * LANES]
            s_ctx = _dot_nt(qs, kc)
            vcs = (with_ones(vc, True), with_ones(vc, False))
            s_loc = vws = None
            if local:
                kw = kvbuf[pl.ds(j0, n_loc), kvh * LANES:(kvh + 1) * LANES]
                vw = kvbuf[pl.ds(j0, n_loc), (N_KV + kvh) * LANES:(N_KV + kvh + 1) * LANES]
                s_loc = _dot_nt(qs, kw)
                vws = (with_ones(vw, True), with_ones(vw, False))
            scores.append((s_ctx, vcs, s_loc, vws))
            fill()
        for kvh in range(N_KV):
            s_ctx, vcs, s_loc, vws = scores[kvh]
            for pair in range(GROUP // 2):
                num, den = [], []
                for half in range(2):
                    g = 2 * pair + half
                    sk = sink_ref[kvh * GROUP + g] * LOG2E
                    sc = s_ctx[g * WINDOW:(g + 1) * WINDOW]
                    if local:
                        sg = s_loc[g * WINDOW:(g + 1) * WINDOW]
                        sl = jnp.concatenate([
                            jnp.where(prev_ok, sg[:, 0:WINDOW], NEG_INF),
                            sg[:, WINDOW:2 * WINDOW],
                            jnp.where(next_ok, sg[:, 2 * WINDOW:3 * WINDOW], NEG_INF)], axis=1)
                        m = jnp.maximum(jnp.maximum(jnp.max(sl, axis=-1, keepdims=True),
                                                    jnp.max(sc, axis=-1, keepdims=True)), sk)
                        o = (_dot(jnp.exp2(sl - m).astype(BF16), vws[half])
                             + _dot(jnp.exp2(sc - m).astype(BF16), vcs[half]))
                    else:
                        m = jnp.maximum(jnp.max(sc, axis=-1, keepdims=True), sk)
                        o = _dot(jnp.exp2(sc - m).astype(BF16), vcs[half])
                    fill()
                    num.append(o[:, 0:LANES])
                    den.append(o[:, LANES:2 * LANES] + jnp.exp2(sk - m))
                o_pair = (num[0] + num[1]) / jnp.where(low_head, den[0], den[1])
                y_parts.append(o_pair.astype(BF16))
        y_parts.append(jnp.concatenate(y_conf, axis=0))
        while fillers:
            fill()
        return jnp.concatenate(y_parts, axis=1)

    return mixer_rows


def _mix_kernel(local, tb, seq,
                x_ref, cf_ref, cfp_ref, cfn_ref, q_ref, kv_ref, kvp_ref, kvn_ref, ckv_ref,
                wa_ref, wc_ref, cb_ref, lg_ref, lb_ref, sink_ref, wo_ref, g1_ref,
                o_ref, cbuf, kvbuf, ybuf):
    mixer_rows = _mixer_setup(local, tb, seq, pl.program_id(1), cf_ref, cfp_ref, cfn_ref, q_ref, kv_ref,
                              kvp_ref, kvn_ref, ckv_ref, wa_ref, wc_ref, cb_ref, lg_ref, lb_ref, sink_ref,
                              cbuf, kvbuf)

    def project(j):
        j0 = pl.multiple_of(j * WINDOW, WINDOW)
        out = _dot(ybuf[pl.ds(j0, WINDOW), :], wo_ref[...])
        o_ref[0, pl.ds(j0, WINDOW), :] = x_ref[0, pl.ds(j0, WINDOW), :] + g1_ref[0] * out

    n_blocks = tb // WINDOW
    ybuf[0:WINDOW, :] = mixer_rows(0)

    def step(j, carry):
        ybuf[pl.ds(pl.multiple_of(j * WINDOW, WINDOW), WINDOW), :] = mixer_rows(
            j, [functools.partial(project, j - 1)])
        return carry

    lax.fori_loop(1, n_blocks, step, 0)
    project(n_blocks - 1)


def _mix(x, cf, q, kv, ckv, mods, mod_row, conv_a_w, conv_c_w, conv_c_b, ln_g, ln_b, sink, w_out_b,
         tb, local):
    b, n, _ = x.shape
    n_ctx = ckv.shape[1]
    const = lambda bb, i: (0, 0)
    hb = tb // HALO
    wb = tb // WINDOW
    kern = functools.partial(_mix_kernel, local, tb, n)
    return pl.pallas_call(
        kern,
        grid=(b, n // tb),
        in_specs=[
            pl.BlockSpec((1, tb, D_MODEL), lambda bb, i: (bb, i, 0)),
            pl.BlockSpec((1, tb, 3 * D_CONV), lambda bb, i: (bb, i, 0)),
            pl.BlockSpec((1, HALO, 3 * D_CONV), lambda bb, i: (bb, jnp.maximum(i * hb - 1, 0), 0)),
            pl.BlockSpec((1, HALO, 3 * D_CONV),
                         lambda bb, i: (bb, jnp.minimum((i + 1) * hb, n // HALO - 1), 0)),
            pl.BlockSpec((1, tb, D_ATTN), lambda bb, i: (bb, i, 0)),
            pl.BlockSpec((1, tb, 4 * LANES), lambda bb, i: (bb, i, 0)),
            pl.BlockSpec((1, WINDOW, 4 * LANES), lambda bb, i: (bb, jnp.maximum(i * wb - 1, 0), 0)),
            pl.BlockSpec((1, WINDOW, 4 * LANES),
                         lambda bb, i: (bb, jnp.minimum((i + 1) * wb, n // WINDOW - 1), 0)),
            pl.BlockSpec((1, n_ctx, 4 * LANES), lambda bb, i: (bb, 0, 0)),
            pl.BlockSpec((SHORT_W, D_CONV), const),
            pl.BlockSpec((CONF_W, D_CONV), const),
            pl.BlockSpec((1, D_CONV), const),
            pl.BlockSpec((1, D_CONV), const),
            pl.BlockSpec((1, D_CONV), const),
            pl.BlockSpec(memory_space=pltpu.SMEM),
            pl.BlockSpec((D_MODEL, D_MODEL), const),
            pl.BlockSpec((1, 1, D_MODEL), lambda bb, i: (mod_row(bb), 0, 2)),
        ],
        out_specs=pl.BlockSpec((1, tb, D_MODEL), lambda bb, i: (bb, i, 0)),
        out_shape=jax.ShapeDtypeStruct((b, n, D_MODEL), F32),
        scratch_shapes=[
            pltpu.VMEM((tb + 2 * HALO, 2 * D_CONV), F32),
            pltpu.VMEM((tb + 2 * WINDOW, 4 * LANES), BF16),
            pltpu.VMEM((tb, D_MODEL), BF16),
        ],
        compiler_params=pltpu.CompilerParams(
            dimension_semantics=("parallel", "parallel"), vmem_limit_bytes=VMEM_LIMIT),
        name="mix_local" if local else "mix_ctx",
    )(x, cf, cf, cf, q, kv, kv, kv, ckv, conv_a_w, conv_c_w, conv_c_b, ln_g, ln_b, sink, w_out_b, mods)


def _mlp_kernel(x_ref, sh_ref, sc_ref, gate_ref, g_ref, w1_ref, w2_ref, o_ref):
    x = x_ref[0]
    ms = jnp.mean(x * x, axis=-1, keepdims=True)
    h = (x * lax.rsqrt(ms + EPS)) * g_ref[...]
    h = h * (1.0 + sc_ref[0]) + sh_ref[0]
    hb = h.astype(BF16)
    acc = None
    for c in range(D_FF // FF_CHUNK):
        hid = _dot(hb, w1_ref[:, c * FF_CHUNK:(c + 1) * FF_CHUNK])
        r = jnp.maximum(hid, 0.0)
        part = _dot((r * r).astype(BF16), w2_ref[c * FF_CHUNK:(c + 1) * FF_CHUNK, :])
        acc = part if acc is None else acc + part
    o_ref[0] = x + gate_ref[0] * acc


def _mlp(x, mods, mod_row, norm_g, w1_b, w2_b, tb):
    b, n, _ = x.shape
    const = lambda bb, i: (0, 0)
    return pl.pallas_call(
        _mlp_kernel,
        grid=(b, n // tb),
        in_specs=[
            pl.BlockSpec((1, tb, D_MODEL), lambda bb, i: (bb, i, 0)),
            pl.BlockSpec((1, 1, D_MODEL), lambda bb, i: (mod_row(bb), 0, 3)),
            pl.BlockSpec((1, 1, D_MODEL), lambda bb, i: (mod_row(bb), 0, 4)),
            pl.BlockSpec((1, 1, D_MODEL), lambda bb, i: (mod_row(bb), 0, 5)),
            pl.BlockSpec((1, D_MODEL), const),
            pl.BlockSpec((D_MODEL, D_FF), const),
            pl.BlockSpec((D_FF, D_MODEL), const),
        ],
        out_specs=pl.BlockSpec((1, tb, D_MODEL), lambda bb, i: (bb, i, 0)),
        out_shape=jax.ShapeDtypeStruct((b, n, D_MODEL), F32),
        compiler_params=pltpu.CompilerParams(
            dimension_semantics=("parallel", "parallel"), vmem_limit_bytes=VMEM_LIMIT),
        name="mlp",
    )(x, mods, mods, mods, norm_g, w1_b, w2_b)


def _layer_kernel(tb, seq,
                  x_ref, cf_ref, cfp_ref, cfn_ref, q_ref, kv_ref, kvp_ref, kvn_ref, ckv_ref,
                  wa_ref, wc_ref, cb_ref, lg_ref, lb_ref, sink_ref, wo_ref, g1_ref,
                  sh2_ref, sc2_ref, g2_ref, n2_ref, w1_ref, w2_ref,
                  o_ref, cbuf, kvbuf, ybuf, x1buf, hbuf, abuf, acc):
    s = pl.program_id(0)
    n_tok = seq // tb
    cur = jnp.minimum(s, pl.num_programs(0) - 2)
    slot_w = lax.rem(s, 2)
    slot_r = 1 - slot_w

    @pl.when(s == 0)
    def _():
        x1buf[1] = jnp.zeros((tb, D_MODEL), F32)
        hbuf[1] = jnp.zeros((tb, D_MODEL), BF16)

    mixer_rows = _mixer_setup(True, tb, seq, lax.rem(cur, n_tok), cf_ref, cfp_ref, cfn_ref, q_ref, kv_ref,
                              kvp_ref, kvn_ref, ckv_ref, wa_ref, wc_ref, cb_ref, lg_ref, lb_ref, sink_ref,
                              cbuf, kvbuf)

    def project(j):
        j0 = pl.multiple_of(j * WINDOW, WINDOW)
        out = _dot(ybuf[pl.ds(j0, WINDOW), :], wo_ref[...])
        x1 = x_ref[0, pl.ds(j0, WINDOW), :] + g1_ref[0] * out
        x1buf[slot_w, pl.ds(j0, WINDOW), :] = x1
        ms = jnp.mean(x1 * x1, axis=-1, keepdims=True)
        h = (x1 * lax.rsqrt(ms + EPS)) * n2_ref[...]
        hbuf[slot_w, pl.ds(j0, WINDOW), :] = (h * (1.0 + sc2_ref[0]) + sh2_ref[0]).astype(BF16)

    def up_proj(c, p):
        hid = _dot(hbuf[slot_r], w1_ref[c, :, p * MLP_COLS:(p + 1) * MLP_COLS])
        r = jnp.maximum(hid, 0.0)
        abuf[:, p * MLP_COLS:(p + 1) * MLP_COLS] = (r * r).astype(BF16)

    def down_proj(c, p, first):
        rows = pl.ds(pl.multiple_of(c * FF_CHUNK, FF_CHUNK), FF_CHUNK)
        part = _dot(abuf[...], w2_ref[rows, p * MLP_COLS:(p + 1) * MLP_COLS])
        cols = slice(p * MLP_COLS, (p + 1) * MLP_COLS)
        acc[:, cols] = part if first else acc[:, cols] + part

    def mlp_fillers(c, first):
        n_p = FF_CHUNK // MLP_COLS
        ups = [functools.partial(up_proj, c, p) for p in range(n_p)]
        downs = [functools.partial(down_proj, c, p, first) for p in range(n_p)]
        return ups + downs

    n_blocks = tb // WINDOW
    n_up = FF_CHUNK // MLP_COLS
    fl = mlp_fillers(0, True)
    ybuf[0:WINDOW, :] = mixer_rows(0, fl[:n_up] + [None] + fl[n_up:])

    def step(j, carry):
        fl = mlp_fillers(j, False)
        ybuf[pl.ds(pl.multiple_of(j * WINDOW, WINDOW), WINDOW), :] = mixer_rows(
            j, fl[:n_up] + [functools.partial(project, j - 1)] + fl[n_up:])
        return carry

    lax.fori_loop(1, n_blocks, step, 0)
    project(n_blocks - 1)
    o_ref[0] = x1buf[slot_r] + g2_ref[0] * acc[...]


def _layer(x, cf, q, kv, ckv, mods, conv_a_w, conv_c_w, conv_c_b, ln_g, ln_b, sink, w_out_b,
           norm2_g, w1_c, w2_b, tb):
    b, n, _ = x.shape
    n_ctx = ckv.shape[1]
    n_tok = n // tb
    n_steps = b * n_tok
    assert tb // WINDOW == D_FF // FF_CHUNK
    hb = tb // HALO
    wb = tb // WINDOW
    const2 = lambda s: (0, 0)
    const3 = lambda s: (0, 0, 0)

    def cur(s):
        c = jnp.minimum(s, n_steps - 1)
        return c // n_tok, c % n_tok

    def prev(s):
        p = jnp.maximum(s - 1, 0)
        return p // n_tok, p % n_tok

    def at_cur(f):
        return lambda s: f(*cur(s))

    single = pl.Buffered(1)
    kern = functools.partial(_layer_kernel, tb, n)
    return pl.pallas_call(
        kern,
        grid=(n_steps + 1,),
        in_specs=[
            pl.BlockSpec((1, tb, D_MODEL), at_cur(lambda bb, i: (bb, i, 0))),
            pl.BlockSpec((1, tb, 3 * D_CONV), at_cur(lambda bb, i: (bb, i, 0))),
            pl.BlockSpec((1, HALO, 3 * D_CONV), at_cur(lambda bb, i: (bb, jnp.maximum(i * hb - 1, 0), 0))),
            pl.BlockSpec((1, HALO, 3 * D_CONV),
                         at_cur(lambda bb, i: (bb, jnp.minimum((i + 1) * hb, n // HALO - 1), 0))),
            pl.BlockSpec((1, tb, D_ATTN), at_cur(lambda bb, i: (bb, i, 0))),
            pl.BlockSpec((1, tb, 4 * LANES), at_cur(lambda bb, i: (bb, i, 0))),
            pl.BlockSpec((1, WINDOW, 4 * LANES), at_cur(lambda bb, i: (bb, jnp.maximum(i * wb - 1, 0), 0))),
            pl.BlockSpec((1, WINDOW, 4 * LANES),
                         at_cur(lambda bb, i: (bb, jnp.minimum((i + 1) * wb, n // WINDOW - 1), 0))),
            pl.BlockSpec((1, n_ctx, 4 * LANES), at_cur(lambda bb, i: (bb, 0, 0))),
            pl.BlockSpec((SHORT_W, D_CONV), const2),
            pl.BlockSpec((CONF_W, D_CONV), const2),
            pl.BlockSpec((1, D_CONV), const2),
            pl.BlockSpec((1, D_CONV), const2),
            pl.BlockSpec((1, D_CONV), const2),
            pl.BlockSpec(memory_space=pltpu.SMEM),
            pl.BlockSpec((D_MODEL, D_MODEL), const2, pipeline_mode=single),
            pl.BlockSpec((1, 1, D_MODEL), at_cur(lambda bb, i: (bb, 0, 2))),
            pl.BlockSpec((1, 1, D_MODEL), at_cur(lambda bb, i: (bb, 0, 3))),
            pl.BlockSpec((1, 1, D_MODEL), at_cur(lambda bb, i: (bb, 0, 4))),
            pl.BlockSpec((1, 1, D_MODEL), lambda s: (prev(s)[0], 0, 5)),
            pl.BlockSpec((1, D_MODEL), const2),
            pl.BlockSpec((D_FF // FF_CHUNK, D_MODEL, FF_CHUNK), const3, pipeline_mode=single),
            pl.BlockSpec((D_FF, D_MODEL), const2, pipeline_mode=single),
        ],
        out_specs=pl.BlockSpec((1, tb, D_MODEL), lambda s: (*prev(s), 0)),
        out_shape=jax.ShapeDtypeStruct((b, n, D_MODEL), F32),
        scratch_shapes=[
            pltpu.VMEM((tb + 2 * HALO, 2 * D_CONV), F32),
            pltpu.VMEM((tb + 2 * WINDOW, 4 * LANES), BF16),
            pltpu.VMEM((tb, D_MODEL), BF16),
            pltpu.VMEM((2, tb, D_MODEL), F32),
            pltpu.VMEM((2, tb, D_MODEL), BF16),
            pltpu.VMEM((tb, FF_CHUNK), BF16),
            pltpu.VMEM((tb, D_MODEL), F32),
        ],
        compiler_params=pltpu.CompilerParams(
            dimension_semantics=("arbitrary",), vmem_limit_bytes=VMEM_LIMIT),
        name="layer",
    )(x, cf, cf, cf, q, kv, kv, kv, ckv, conv_a_w, conv_c_w, conv_c_b, ln_g, ln_b, sink, w_out_b,
      mods, mods, mods, mods, norm2_g, w1_c, w2_b)


def _rope_tables(seq):
    d_axis = HEAD_DIM // 2
    inv_freq = ROPE_BASE ** (-jnp.arange(0, d_axis, 2, dtype=F32) / d_axis)
    pos = jnp.arange(seq, dtype=jnp.int32)
    ang_r = (pos // GRID_W).astype(F32)[:, None] * inv_freq[None, :]
    ang_c = (pos % GRID_W).astype(F32)[:, None] * inv_freq[None, :]
    cos = jnp.concatenate([jnp.cos(ang_r), jnp.cos(ang_r), jnp.cos(ang_c), jnp.cos(ang_c)], axis=-1)
    sin = jnp.concatenate([-jnp.sin(ang_r), jnp.sin(ang_r), -jnp.sin(ang_c), jnp.sin(ang_c)], axis=-1)
    reps = LANES // HEAD_DIM
    return jnp.tile(cos, (1, reps)), jnp.tile(sin, (1, reps))


def kernel(x, c, ctx, c_ctx, w_mod, b_mod, norm1_g, w_in, conv_a_w, q_norm_g, k_norm_g, attn_sink,
           conv_c_w, conv_c_b, ln_c_g, ln_c_b, w_out, norm2_g, w_mlp1, w_mlp2):
    batch, seq, _ = x.shape
    n_ctx = ctx.shape[1]
    depth = w_mod.shape[0]
    tb = min(TOKEN_BLOCK, seq)
    tb_ctx = n_ctx
    ctx_row = batch

    cvec = jnp.concatenate([c, c_ctx[None, :], jnp.zeros((8 - batch - 1, D_MODEL), F32)], axis=0)
    mods_all = _modulation(cvec, w_mod, b_mod).reshape(depth, 8, 1, 6 * D_MODEL)

    cos_t, sin_t = _rope_tables(seq)
    cos_c = jnp.ones((n_ctx, LANES), F32)
    sin_c = jnp.zeros((n_ctx, LANES), F32)
    reps = LANES // HEAD_DIM
    lat_row = lambda bb: bb
    cx_row = lambda bb: ctx_row

    xc = ctx
    for l in range(depth):
        last = l == depth - 1
        mods = mods_all[l]
        w_in_b = w_in[l].astype(BF16)
        w_out_b = w_out[l].astype(BF16)
        w1_b = w_mlp1[l].astype(BF16)
        w2_b = w_mlp2[l].astype(BF16)
        g1n = norm1_g[l][None, :]
        g2n = norm2_g[l][None, :]
        gq = jnp.tile(q_norm_g[l], reps)[None, :]
        gk = jnp.tile(k_norm_g[l], reps)[None, :]
        conv_args = (conv_a_w[l], conv_c_w[l], conv_c_b[l][None, :], ln_c_g[l][None, :],
                     ln_c_b[l][None, :], attn_sink[l], w_out_b)

        cf_c, q_c, kv_c = _proj(xc, mods, cx_row, g1n, w_in_b, cos_c, sin_c, gq, gk, tb_ctx)
        cf, q, kv = _proj(x, mods, lat_row, g1n, w_in_b, cos_t, sin_t, gq, gk, tb)
        w1_c = w1_b.reshape(D_MODEL, D_FF // FF_CHUNK, FF_CHUNK).transpose(1, 0, 2)
        x = _layer(x, cf, q, kv, kv_c, mods, *conv_args, g2n, w1_c, w2_b, tb)
        if not last:
            xc = _mix(xc, cf_c, q_c, kv_c, kv_c, mods, cx_row, *conv_args, tb=tb_ctx, local=False)
            xc = _mlp(xc, mods, cx_row, g2n, w1_b, w2_b, tb_ctx)
    return x
```

```python
import functools

import jax
import jax.numpy as jnp
from jax import lax
from jax.experimental import pallas as pl
from jax.experimental.pallas import tpu as pltpu

F32 = jnp.float32
BF16 = jnp.bfloat16

D_MODEL = 1024
HEAD_DIM = 64
N_HEADS = 8
N_KV = 2
GROUP = N_HEADS // N_KV
D_CONV = 256
D_ATTN = N_HEADS * HEAD_DIM
D_FF = 4 * D_MODEL
GRID_W = 64
WINDOW = 128
SHORT_W = 3
CONF_W = 31
HALO = 16
ROPE_BASE = 10000.0
EPS = 1e-6
NEG_INF = -1e30
LOG2E = 1.4426950408889634

OFF_Q = 3 * D_CONV
OFF_K = OFF_Q + D_ATTN
OFF_C = OFF_K + 2 * N_KV * HEAD_DIM
D_IN = OFF_C + 2 * D_CONV

LANES = 128
TOKEN_BLOCK = 512
CONV_ROWS = 64
FF_CHUNK = 1024
MOD_COLS = 1536
VMEM_LIMIT = 56 * 1024 * 1024


def _sigmoid(t):
    return 1.0 / (1.0 + jnp.exp(-t))


def _dot(a, b):
    return jnp.dot(a, b, preferred_element_type=F32)


def _dot_nt(a, b):
    return lax.dot_general(a, b, (((1,), (1,)), ((), ())), preferred_element_type=F32)


def _mod_kernel(c_ref, w_ref, b_ref, o_ref):
    c = c_ref[...]
    s = c * _sigmoid(c)
    o_ref[0] = jnp.dot(s, w_ref[0], preferred_element_type=F32,
                       precision=lax.Precision.HIGHEST) + b_ref[0]


def _modulation(cvec, w_mod, b_mod):
    depth, _, n = w_mod.shape
    return pl.pallas_call(
        _mod_kernel,
        grid=(depth, n // MOD_COLS),
        in_specs=[
            pl.BlockSpec((8, D_MODEL), lambda l, j: (0, 0)),
            pl.BlockSpec((1, D_MODEL, MOD_COLS), lambda l, j: (l, 0, j)),
            pl.BlockSpec((1, 1, MOD_COLS), lambda l, j: (l, 0, j)),
        ],
        out_specs=pl.BlockSpec((1, 8, MOD_COLS), lambda l, j: (l, 0, j)),
        out_shape=jax.ShapeDtypeStruct((depth, 8, n), F32),
        compiler_params=pltpu.CompilerParams(
            dimension_semantics=("arbitrary", "arbitrary"), vmem_limit_bytes=VMEM_LIMIT),
        name="modulation",
    )(cvec, w_mod, b_mod.reshape(depth, 1, n))


def _phase_conv(buf, col0, w_ref, n_taps, r0, rows):
    phases = {}
    for k in range(n_taps):
        off = HALO + k - n_taps // 2
        phases.setdefault(off % 8, []).append((off - off % 8, k))
    out = None
    for p in sorted(phases):
        n_rows = rows if p == 0 else rows + 8
        z = None
        for a8, k in phases[p]:
            term = buf[r0 + a8:r0 + a8 + n_rows, col0:col0 + D_CONV] * w_ref[k:k + 1, :]
            z = term if z is None else z + term
        z = z[p:p + rows]
        out = z if out is None else out + z
    return out


def _proj_kernel(tb, seq,
                 x_ref, xp_ref, xn_ref, sh_ref, sc_ref, g_ref, w_ref, cos_ref, sin_ref, gq_ref, gk_ref,
                 wa_ref, wc_ref, cb_ref, lg_ref, lb_ref,
                 y_ref, q_ref, kv_ref, cbuf):
    i = pl.program_id(1)
    has_prev = i > 0
    has_next = i < seq // tb - 1

    def norm_mod(x):
        ms = jnp.mean(x * x, axis=-1, keepdims=True)
        h = (x * lax.rsqrt(ms + EPS)) * g_ref[...]
        return (h * (1.0 + sc_ref[0]) + sh_ref[0]).astype(BF16)

    hb = jnp.concatenate([norm_mod(xp_ref[0]), norm_mod(x_ref[0]), norm_mod(xn_ref[0])], axis=0)
    hb_main = hb[HALO:HALO + tb]
    rows_all = tb + 2 * HALO
    row = lax.broadcasted_iota(jnp.int32, (rows_all, 1), 0)
    inside = ((row >= HALO) | has_prev) & ((row < HALO + tb) | has_next)

    ua = _dot(hb, w_ref[:, 0:OFF_Q])
    cbuf[:, 0:D_CONV] = jnp.where(inside, ua[:, 2 * D_CONV:3 * D_CONV] * ua[:, 0:D_CONV], 0.0)
    b_gate = ua[HALO:HALO + tb, D_CONV:2 * D_CONV]

    uc = _dot(hb, w_ref[:, OFF_C:D_IN])
    cbuf[:, D_CONV:2 * D_CONV] = jnp.where(inside, uc[:, 0:D_CONV] * _sigmoid(uc[:, D_CONV:2 * D_CONV]), 0.0)

    lane = lax.broadcasted_iota(jnp.int32, (1, LANES), 1)
    rot_from_above = (lane % (HEAD_DIM // 2)) < (HEAD_DIM // 4)
    low_head = lane < HEAD_DIM
    r_i = lax.broadcasted_iota(jnp.int32, (LANES, LANES), 0) // HEAD_DIM
    c_i = lax.broadcasted_iota(jnp.int32, (LANES, LANES), 1) // HEAD_DIM
    head_ones = jnp.where(r_i == c_i, 1.0, 0.0).astype(BF16)
    cos = cos_ref[...]
    sin = sin_ref[...]

    def norm_rope(t, gain, scale):
        ssq = _dot((t * t).astype(BF16), head_ones)
        rs = lax.rsqrt(ssq * (1.0 / HEAD_DIM) + EPS) * scale
        tg = t * gain
        partner = jnp.where(rot_from_above, pltpu.roll(tg, LANES - HEAD_DIM // 4, 1),
                            pltpu.roll(tg, HEAD_DIM // 4, 1))
        return rs * (tg * cos + partner * sin)

    uq = _dot(hb_main, w_ref[:, OFF_Q:OFF_K])
    gq = gq_ref[...]
    for s in range(D_ATTN // LANES):
        qs = norm_rope(uq[:, s * LANES:(s + 1) * LANES], gq, LOG2E * HEAD_DIM ** -0.5)
        q_ref[0, :, s * LANES:(s + 1) * LANES] = qs.astype(BF16)

    ukv = _dot(hb_main, w_ref[:, OFF_K:OFF_C])
    kn = norm_rope(ukv[:, 0:LANES], gk_ref[...], 1.0)
    for j, t in enumerate((kn, ukv[:, LANES:2 * LANES])):
        swapped = pltpu.roll(t, HEAD_DIM, 1)
        kv_ref[0, :, (2 * j) * LANES:(2 * j + 1) * LANES] = jnp.where(low_head, t, swapped).astype(BF16)
        kv_ref[0, :, (2 * j + 1) * LANES:(2 * j + 2) * LANES] = jnp.where(low_head, swapped, t).astype(BF16)

    for c in range(tb // CONV_ROWS):
        r0 = c * CONV_ROWS
        a = _phase_conv(cbuf, 0, wa_ref, SHORT_W, r0, CONV_ROWS)
        y_ref[0, r0:r0 + CONV_ROWS, 0:D_CONV] = (b_gate[r0:r0 + CONV_ROWS] * a).astype(BF16)
        acc = _phase_conv(cbuf, D_CONV, wc_ref, CONF_W, r0, CONV_ROWS) + cb_ref[...]
        mu = jnp.mean(acc, axis=-1, keepdims=True)
        xc = acc - mu
        var = jnp.mean(xc * xc, axis=-1, keepdims=True)
        yn = (xc * lax.rsqrt(var + EPS)) * lg_ref[...] + lb_ref[...]
        y_ref[0, r0:r0 + CONV_ROWS, D_CONV:2 * D_CONV] = (yn * _sigmoid(yn)).astype(BF16)


def _proj(x, mods, mod_row, norm_g, w_in_b, cos_t, sin_t, gq, gk, conv_a_w, conv_c_w, conv_c_b, ln_g, ln_b, tb):
    b, n, _ = x.shape
    const = lambda bb, i: (0, 0)
    hb = tb // HALO
    kern = functools.partial(_proj_kernel, tb, n)
    return pl.pallas_call(
        kern,
        grid=(b, n // tb),
        in_specs=[
            pl.BlockSpec((1, tb, D_MODEL), lambda bb, i: (bb, i, 0)),
            pl.BlockSpec((1, HALO, D_MODEL), lambda bb, i: (bb, jnp.maximum(i * hb - 1, 0), 0)),
            pl.BlockSpec((1, HALO, D_MODEL), lambda bb, i: (bb, jnp.minimum((i + 1) * hb, n // HALO - 1), 0)),
            pl.BlockSpec((1, 1, D_MODEL), lambda bb, i: (mod_row(bb), 0, 0)),
            pl.BlockSpec((1, 1, D_MODEL), lambda bb, i: (mod_row(bb), 0, 1)),
            pl.BlockSpec((1, D_MODEL), const),
            pl.BlockSpec((D_MODEL, D_IN), const),
            pl.BlockSpec((tb, LANES), lambda bb, i: (i, 0)),
            pl.BlockSpec((tb, LANES), lambda bb, i: (i, 0)),
            pl.BlockSpec((1, LANES), const),
            pl.BlockSpec((1, LANES), const),
            pl.BlockSpec((SHORT_W, D_CONV), const),
            pl.BlockSpec((CONF_W, D_CONV), const),
            pl.BlockSpec((1, D_CONV), const),
            pl.BlockSpec((1, D_CONV), const),
            pl.BlockSpec((1, D_CONV), const),
        ],
        out_specs=[
            pl.BlockSpec((1, tb, 2 * D_CONV), lambda bb, i: (bb, i, 0)),
            pl.BlockSpec((1, tb, D_ATTN), lambda bb, i: (bb, i, 0)),
            pl.BlockSpec((1, tb, 4 * LANES), lambda bb, i: (bb, i, 0)),
        ],
        out_shape=[
            jax.ShapeDtypeStruct((b, n, 2 * D_CONV), BF16),
            jax.ShapeDtypeStruct((b, n, D_ATTN), BF16),
            jax.ShapeDtypeStruct((b, n, 4 * LANES), BF16),
        ],
        scratch_shapes=[pltpu.VMEM((tb + 2 * HALO, 2 * D_CONV), F32)],
        compiler_params=pltpu.CompilerParams(
            dimension_semantics=("parallel", "parallel"), vmem_limit_bytes=VMEM_LIMIT),
        name="proj",
    )(x, x, x, mods, mods, norm_g, w_in_b, cos_t, sin_t, gq, gk, conv_a_w, conv_c_w, conv_c_b, ln_g, ln_b)


def _mix_kernel(local, tb, seq,
                x_ref, y_ref, q_ref, kv_ref, kvp_ref, kvn_ref, ckv_ref, sink_ref, wo_ref, g1_ref,
                o_ref, kvbuf, ybuf):
    i = pl.program_id(1)
    lane = lax.broadcasted_iota(jnp.int32, (1, LANES), 1)
    low_head = lane < HEAD_DIM
    if local:
        kvbuf[0:WINDOW, :] = kvp_ref[0]
        kvbuf[WINDOW:WINDOW + tb, :] = kv_ref[0]
        kvbuf[WINDOW + tb:WINDOW + tb + WINDOW, :] = kvn_ref[0]
        n_loc = 3 * WINDOW
        rel = (lax.broadcasted_iota(jnp.int32, (WINDOW, WINDOW), 1)
               - lax.broadcasted_iota(jnp.int32, (WINDOW, WINDOW), 0))
    ones_cols = jnp.ones((1, LANES), BF16)

    def with_ones(v, keep_low):
        zero = jnp.zeros_like(v)
        masked = jnp.where(low_head, v, zero) if keep_low else jnp.where(low_head, zero, v)
        return jnp.concatenate([masked, jnp.broadcast_to(ones_cols, v.shape)], axis=1)

    def project(j):
        j0 = pl.multiple_of(j * WINDOW, WINDOW)
        out = _dot(ybuf[pl.ds(j0, WINDOW), :], wo_ref[...])
        o_ref[0, pl.ds(j0, WINDOW), :] = x_ref[0, pl.ds(j0, WINDOW), :] + g1_ref[0] * out

    def mixer_rows(j, prev=None):
        j0 = pl.multiple_of(j * WINDOW, WINDOW)
        yj = y_ref[0, pl.ds(j0, WINDOW), :]
        y_parts = [yj[:, 0:D_CONV]]

        qj = q_ref[0, pl.ds(j0, WINDOW), :]
        if local:
            base = i * tb + j0
            big = jnp.int32(1 << 20)
            prev_ok = rel >= jnp.where(base >= WINDOW, 0, big)
            next_ok = rel <= jnp.where(base + 2 * WINDOW <= seq, 0, -big)
        scores = []
        for kvh in range(N_KV):
            parts = []
            for g in range(GROUP):
                h = kvh * GROUP + g
                slab = qj[:, (h // 2) * LANES:(h // 2 + 1) * LANES]
                parts.append(jnp.where(low_head if h % 2 == 0 else jnp.logical_not(low_head),
                                       slab, jnp.zeros_like(slab)))
            qs = jnp.concatenate(parts, axis=0)
            kc = ckv_ref[0, :, kvh * LANES:(kvh + 1) * LANES]
            vc = ckv_ref[0, :, (N_KV + kvh) * LANES:(N_KV + kvh + 1) * LANES]
            s_ctx = _dot_nt(qs, kc)
            vcs = (with_ones(vc, True), with_ones(vc, False))
            s_loc = vws = None
            if local:
                kw = kvbuf[pl.ds(j0, n_loc), kvh * LANES:(kvh + 1) * LANES]
                vw = kvbuf[pl.ds(j0, n_loc), (N_KV + kvh) * LANES:(N_KV + kvh + 1) * LANES]
                s_loc = _dot_nt(qs, kw)
                vws = (with_ones(vw, True), with_ones(vw, False))
            scores.append((s_ctx, vcs, s_loc, vws))
            if kvh == 0 and prev is not None:
                project(prev)
        for kvh in range(N_KV):
            s_ctx, vcs, s_loc, vws = scores[kvh]
            for pair in range(GROUP // 2):
                num, den = [], []
                for half in range(2):
                    g = 2 * pair + half
                    sk = sink_ref[kvh * GROUP + g] * LOG2E
                    sc = s_ctx[g * WINDOW:(g + 1) * WINDOW]
                    if local:
                        sg = s_loc[g * WINDOW:(g + 1) * WINDOW]
                        sl = jnp.concatenate([
                            jnp.where(prev_ok, sg[:, 0:WINDOW], NEG_INF),
                            sg[:, WINDOW:2 * WINDOW],
                            jnp.where(next_ok, sg[:, 2 * WINDOW:3 * WINDOW], NEG_INF)], axis=1)
                        m = jnp.maximum(jnp.maximum(jnp.max(sl, axis=-1, keepdims=True),
                                                    jnp.max(sc, axis=-1, keepdims=True)), sk)
                        o = (_dot(jnp.exp2(sl - m).astype(BF16), vws[half])
                             + _dot(jnp.exp2(sc - m).astype(BF16), vcs[half]))
                    else:
                        m = jnp.maximum(jnp.max(sc, axis=-1, keepdims=True), sk)
                        o = _dot(jnp.exp2(sc - m).astype(BF16), vcs[half])
                    num.append(o[:, 0:LANES])
                    den.append(o[:, LANES:2 * LANES] + jnp.exp2(sk - m))
                o_pair = (num[0] + num[1]) / jnp.where(low_head, den[0], den[1])
                y_parts.append(o_pair.astype(BF16))
        y_parts.append(yj[:, D_CONV:2 * D_CONV])
        return jnp.concatenate(y_parts, axis=1)

    n_blocks = tb // WINDOW
    ybuf[0:WINDOW, :] = mixer_rows(0)

    def step(j, carry):
        ybuf[pl.ds(pl.multiple_of(j * WINDOW, WINDOW), WINDOW), :] = mixer_rows(j, j - 1)
        return carry

    lax.fori_loop(1, n_blocks, step, 0)
    project(n_blocks - 1)


def _mix(x, y, q, kv, ckv, mods, mod_row, sink, w_out_b, tb, local):
    b, n, _ = x.shape
    n_ctx = ckv.shape[1]
    const = lambda bb, i: (0, 0)
    wb = tb // WINDOW
    kern = functools.partial(_mix_kernel, local, tb, n)
    return pl.pallas_call(
        kern,
        grid=(b, n // tb),
        in_specs=[
            pl.BlockSpec((1, tb, D_MODEL), lambda bb, i: (bb, i, 0)),
            pl.BlockSpec((1, tb, 2 * D_CONV), lambda bb, i: (bb, i, 0)),
            pl.BlockSpec((1, tb, D_ATTN), lambda bb, i: (bb, i, 0)),
            pl.BlockSpec((1, tb, 4 * LANES), lambda bb, i: (bb, i, 0)),
            pl.BlockSpec((1, WINDOW, 4 * LANES), lambda bb, i: (bb, jnp.maximum(i * wb - 1, 0), 0)),
            pl.BlockSpec((1, WINDOW, 4 * LANES),
                         lambda bb, i: (bb, jnp.minimum((i + 1) * wb, n // WINDOW - 1), 0)),
            pl.BlockSpec((1, n_ctx, 4 * LANES), lambda bb, i: (bb, 0, 0)),
            pl.BlockSpec(memory_space=pltpu.SMEM),
            pl.BlockSpec((D_MODEL, D_MODEL), const),
            pl.BlockSpec((1, 1, D_MODEL), lambda bb, i: (mod_row(bb), 0, 2)),
        ],
        out_specs=pl.BlockSpec((1, tb, D_MODEL), lambda bb, i: (bb, i, 0)),
        out_shape=jax.ShapeDtypeStruct((b, n, D_MODEL), F32),
        scratch_shapes=[
            pltpu.VMEM((tb + 2 * WINDOW, 4 * LANES), BF16),
            pltpu.VMEM((tb, D_MODEL), BF16),
        ],
        compiler_params=pltpu.CompilerParams(
            dimension_semantics=("parallel", "parallel"), vmem_limit_bytes=VMEM_LIMIT),
        name="mix_local" if local else "mix_ctx",
    )(x, y, q, kv, kv, kv, ckv, sink, w_out_b, mods)


def _mlp_kernel(x_ref, sh_ref, sc_ref, gate_ref, g_ref, w1_ref, w2_ref, o_ref):
    x = x_ref[0]
    ms = jnp.mean(x * x, axis=-1, keepdims=True)
    h = (x * lax.rsqrt(ms + EPS)) * g_ref[...]
    h = h * (1.0 + sc_ref[0]) + sh_ref[0]
    hb = h.astype(BF16)
    acc = None
    for c in range(D_FF // FF_CHUNK):
        hid = _dot(hb, w1_ref[:, c * FF_CHUNK:(c + 1) * FF_CHUNK])
        r = jnp.maximum(hid, 0.0)
        part = _dot((r * r).astype(BF16), w2_ref[c * FF_CHUNK:(c + 1) * FF_CHUNK, :])
        acc = part if acc is None else acc + part
    o_ref[0] = x + gate_ref[0] * acc


def _mlp(x, mods, mod_row, norm_g, w1_b, w2_b, tb):
    b, n, _ = x.shape
    const = lambda bb, i: (0, 0)
    return pl.pallas_call(
        _mlp_kernel,
        grid=(b, n // tb),
        in_specs=[
            pl.BlockSpec((1, tb, D_MODEL), lambda bb, i: (bb, i, 0)),
            pl.BlockSpec((1, 1, D_MODEL), lambda bb, i: (mod_row(bb), 0, 3)),
            pl.BlockSpec((1, 1, D_MODEL), lambda bb, i: (mod_row(bb), 0, 4)),
            pl.BlockSpec((1, 1, D_MODEL), lambda bb, i: (mod_row(bb), 0, 5)),
            pl.BlockSpec((1, D_MODEL), const),
            pl.BlockSpec((D_MODEL, D_FF), const),
            pl.BlockSpec((D_FF, D_MODEL), const),
        ],
        out_specs=pl.BlockSpec((1, tb, D_MODEL), lambda bb, i: (bb, i, 0)),
        out_shape=jax.ShapeDtypeStruct((b, n, D_MODEL), F32),
        compiler_params=pltpu.CompilerParams(
            dimension_semantics=("parallel", "parallel"), vmem_limit_bytes=VMEM_LIMIT),
        name="mlp",
    )(x, mods, mods, mods, norm_g, w1_b, w2_b)


def _rope_tables(seq):
    d_axis = HEAD_DIM // 2
    inv_freq = ROPE_BASE ** (-jnp.arange(0, d_axis, 2, dtype=F32) / d_axis)
    pos = jnp.arange(seq, dtype=jnp.int32)
    ang_r = (pos // GRID_W).astype(F32)[:, None] * inv_freq[None, :]
    ang_c = (pos % GRID_W).astype(F32)[:, None] * inv_freq[None, :]
    cos = jnp.concatenate([jnp.cos(ang_r), jnp.cos(ang_r), jnp.cos(ang_c), jnp.cos(ang_c)], axis=-1)
    sin = jnp.concatenate([-jnp.sin(ang_r), jnp.sin(ang_r), -jnp.sin(ang_c), jnp.sin(ang_c)], axis=-1)
    reps = LANES // HEAD_DIM
    return jnp.tile(cos, (1, reps)), jnp.tile(sin, (1, reps))


def kernel(x, c, ctx, c_ctx, w_mod, b_mod, norm1_g, w_in, conv_a_w, q_norm_g, k_norm_g, attn_sink,
           conv_c_w, conv_c_b, ln_c_g, ln_c_b, w_out, norm2_g, w_mlp1, w_mlp2):
    batch, seq, _ = x.shape
    n_ctx = ctx.shape[1]
    depth = w_mod.shape[0]
    tb = min(TOKEN_BLOCK, seq)
    tb_ctx = n_ctx
    ctx_row = batch

    cvec = jnp.concatenate([c, c_ctx[None, :], jnp.zeros((8 - batch - 1, D_MODEL), F32)], axis=0)
    mods_all = _modulation(cvec, w_mod, b_mod).reshape(depth, 8, 1, 6 * D_MODEL)

    cos_t, sin_t = _rope_tables(seq)
    cos_c = jnp.ones((n_ctx, LANES), F32)
    sin_c = jnp.zeros((n_ctx, LANES), F32)
    reps = LANES // HEAD_DIM
    lat_row = lambda bb: bb
    cx_row = lambda bb: ctx_row

    xc = ctx
    for l in range(depth):
        last = l == depth - 1
        mods = mods_all[l]
        w_in_b = w_in[l].astype(BF16)
        w_out_b = w_out[l].astype(BF16)
        w1_b = w_mlp1[l].astype(BF16)
        w2_b = w_mlp2[l].astype(BF16)
        g1n = norm1_g[l][None, :]
        g2n = norm2_g[l][None, :]
        gq = jnp.tile(q_norm_g[l], reps)[None, :]
        gk = jnp.tile(k_norm_g[l], reps)[None, :]
        conv_args = (conv_a_w[l], conv_c_w[l], conv_c_b[l][None, :], ln_c_g[l][None, :], ln_c_b[l][None, :])

        y_c, q_c, kv_c = _proj(xc, mods, cx_row, g1n, w_in_b, cos_c, sin_c, gq, gk, *conv_args, tb_ctx)
        y, q, kv = _proj(x, mods, lat_row, g1n, w_in_b, cos_t, sin_t, gq, gk, *conv_args, tb)
        x = _mix(x, y, q, kv, kv_c, mods, lat_row, attn_sink[l], w_out_b, tb, True)
        x = _mlp(x, mods, lat_row, g2n, w1_b, w2_b, tb)
        if not last:
            xc = _mix(xc, y_c, q_c, kv_c, kv_c, mods, cx_row, attn_sink[l], w_out_b, tb_ctx, False)
            xc = _mlp(xc, mods, cx_row, g2n, w1_b, w2_b, tb_ctx)
    return x
```

```python
import functools

import jax
import jax.numpy as jnp
from jax import lax
from jax.experimental import pallas as pl
from jax.experimental.pallas import tpu as pltpu

F32 = jnp.float32
BF16 = jnp.bfloat16

D_MODEL = 1024
HEAD_DIM = 64
N_HEADS = 8
N_KV = 2
GROUP = N_HEADS // N_KV
D_CONV = 256
D_ATTN = N_HEADS * HEAD_DIM
D_FF = 4 * D_MODEL
GRID_W = 64
WINDOW = 128
SHORT_W = 3
CONF_W = 31
HALO = 16
ROPE_BASE = 10000.0
EPS = 1e-6
NEG_INF = -1e30
LOG2E = 1.4426950408889634

OFF_Q = 3 * D_CONV
OFF_K = OFF_Q + D_ATTN
OFF_C = OFF_K + 2 * N_KV * HEAD_DIM
D_IN = OFF_C + 2 * D_CONV

LANES = 128
TOKEN_BLOCK = 512
WIDE_BLOCK = 1024
CONV_ROWS = 64
FF_CHUNK = 1024
MOD_COLS = 1536
VMEM_LIMIT = 56 * 1024 * 1024


def _sigmoid(t):
    return 1.0 / (1.0 + jnp.exp(-t))


def _dot(a, b):
    return jnp.dot(a, b, preferred_element_type=F32)


def _dot_nt(a, b):
    return lax.dot_general(a, b, (((1,), (1,)), ((), ())), preferred_element_type=F32)


def _mod_kernel(c_ref, w_ref, b_ref, o_ref):
    c = c_ref[...]
    s = c * _sigmoid(c)
    o_ref[0] = jnp.dot(s, w_ref[0], preferred_element_type=F32,
                       precision=lax.Precision.HIGHEST) + b_ref[0]


def _modulation(cvec, w_mod, b_mod):
    depth, _, n = w_mod.shape
    return pl.pallas_call(
        _mod_kernel,
        grid=(depth, n // MOD_COLS),
        in_specs=[
            pl.BlockSpec((8, D_MODEL), lambda l, j: (0, 0)),
            pl.BlockSpec((1, D_MODEL, MOD_COLS), lambda l, j: (l, 0, j)),
            pl.BlockSpec((1, 1, MOD_COLS), lambda l, j: (l, 0, j)),
        ],
        out_specs=pl.BlockSpec((1, 8, MOD_COLS), lambda l, j: (l, 0, j)),
        out_shape=jax.ShapeDtypeStruct((depth, 8, n), F32),
        compiler_params=pltpu.CompilerParams(
            dimension_semantics=("arbitrary", "arbitrary"), vmem_limit_bytes=VMEM_LIMIT),
        name="modulation",
    )(cvec, w_mod, b_mod.reshape(depth, 1, n))


def _phase_conv(buf, col0, w_ref, n_taps, r0, rows):
    phases = {}
    for k in range(n_taps):
        off = HALO + k - n_taps // 2
        phases.setdefault(off % 8, []).append((off - off % 8, k))
    out = None
    for p in sorted(phases):
        n_rows = rows if p == 0 else rows + 8
        z = None
        for a8, k in phases[p]:
            term = buf[r0 + a8:r0 + a8 + n_rows, col0:col0 + D_CONV] * w_ref[k:k + 1, :]
            z = term if z is None else z + term
        z = z[p:p + rows]
        out = z if out is None else out + z
    return out


def _proj_kernel(tb, seq,
                 x_ref, xp_ref, xn_ref, sh_ref, sc_ref, g_ref, w_ref, cos_ref, sin_ref, gq_ref, gk_ref,
                 wa_ref, wc_ref, cb_ref, lg_ref, lb_ref,
                 y_ref, q_ref, kv_ref, cbuf):
    i = pl.program_id(1)
    has_prev = i > 0
    has_next = i < seq // tb - 1

    def norm_mod(x):
        ms = jnp.mean(x * x, axis=-1, keepdims=True)
        h = (x * lax.rsqrt(ms + EPS)) * g_ref[...]
        return (h * (1.0 + sc_ref[0]) + sh_ref[0]).astype(BF16)

    lane = lax.broadcasted_iota(jnp.int32, (1, LANES), 1)
    rot_from_above = (lane % (HEAD_DIM // 2)) < (HEAD_DIM // 4)
    low_head = lane < HEAD_DIM
    r_i = lax.broadcasted_iota(jnp.int32, (LANES, LANES), 0) // HEAD_DIM
    c_i = lax.broadcasted_iota(jnp.int32, (LANES, LANES), 1) // HEAD_DIM
    head_ones = jnp.where(r_i == c_i, 1.0, 0.0).astype(BF16)

    def norm_rope(t, gain, scale, cos, sin):
        ssq = _dot((t * t).astype(BF16), head_ones)
        rs = lax.rsqrt(ssq * (1.0 / HEAD_DIM) + EPS) * scale
        tg = t * gain
        partner = jnp.where(rot_from_above, pltpu.roll(tg, LANES - HEAD_DIM // 4, 1),
                            pltpu.roll(tg, HEAD_DIM // 4, 1))
        return rs * (tg * cos + partner * sin)

    def conv_chunk(c):
        r0 = c * CONV_ROWS
        a = _phase_conv(cbuf, 0, wa_ref, SHORT_W, r0, CONV_ROWS)
        b_gate = cbuf[HALO + r0:HALO + r0 + CONV_ROWS, 2 * D_CONV:3 * D_CONV]
        y_ref[0, r0:r0 + CONV_ROWS, 0:D_CONV] = (b_gate * a).astype(BF16)
        acc = _phase_conv(cbuf, D_CONV, wc_ref, CONF_W, r0, CONV_ROWS) + cb_ref[...]
        mu = jnp.mean(acc, axis=-1, keepdims=True)
        xc = acc - mu
        var = jnp.mean(xc * xc, axis=-1, keepdims=True)
        yn = (xc * lax.rsqrt(var + EPS)) * lg_ref[...] + lb_ref[...]
        y_ref[0, r0:r0 + CONV_ROWS, D_CONV:2 * D_CONV] = (yn * _sigmoid(yn)).astype(BF16)

    half = tb // 2
    pieces = ((0, half + 2 * HALO), (half + 2 * HALO, tb + 2 * HALO))
    main = tuple((max(lo - HALO, 0), min(hi - HALO, tb)) for lo, hi in pieces)

    def normed(p):
        lo, hi = pieces[p]
        parts = [norm_mod(x_ref[0, main[p][0]:main[p][1], :])]
        if lo == 0:
            parts.insert(0, norm_mod(xp_ref[0]))
        if hi == tb + 2 * HALO:
            parts.append(norm_mod(xn_ref[0]))
        return jnp.concatenate(parts, axis=0)

    def main_rows(p, hb):
        return hb[main[p][0] + HALO - pieces[p][0]:main[p][1] + HALO - pieces[p][0]]

    def stage_short(p, hb):
        lo, hi = pieces[p]
        ua = _dot(hb, w_ref[:, 0:OFF_Q])
        cbuf[lo:hi, 0:D_CONV] = ua[:, 2 * D_CONV:3 * D_CONV] * ua[:, 0:D_CONV]
        cbuf[lo:hi, 2 * D_CONV:3 * D_CONV] = ua[:, D_CONV:2 * D_CONV]

    def stage_glu(p, hb):
        lo, hi = pieces[p]
        uc = _dot(hb, w_ref[:, OFF_C:D_IN])
        cbuf[lo:hi, D_CONV:2 * D_CONV] = uc[:, 0:D_CONV] * _sigmoid(uc[:, D_CONV:2 * D_CONV])
        if lo == 0:
            cbuf[0:HALO, 0:2 * D_CONV] = jnp.where(has_prev, cbuf[0:HALO, 0:2 * D_CONV], 0.0)
        if hi == tb + 2 * HALO:
            cbuf[hi - HALO:hi, 0:2 * D_CONV] = jnp.where(has_next, cbuf[hi - HALO:hi, 0:2 * D_CONV], 0.0)

    def q_outputs(p, hb, slabs):
        m_lo, m_hi = main[p]
        s0, s1 = slabs
        uq = _dot(main_rows(p, hb), w_ref[:, OFF_Q + s0 * LANES:OFF_Q + s1 * LANES])
        for s in range(s0, s1):
            qs = norm_rope(uq[:, (s - s0) * LANES:(s - s0 + 1) * LANES], gq_ref[...], LOG2E * HEAD_DIM ** -0.5,
                           cos_ref[m_lo:m_hi, :], sin_ref[m_lo:m_hi, :])
            q_ref[0, m_lo:m_hi, s * LANES:(s + 1) * LANES] = qs.astype(BF16)

    def kv_outputs(p, hb):
        m_lo, m_hi = main[p]
        ukv = _dot(main_rows(p, hb), w_ref[:, OFF_K:OFF_C])
        kn = norm_rope(ukv[:, 0:LANES], gk_ref[...], 1.0, cos_ref[m_lo:m_hi, :], sin_ref[m_lo:m_hi, :])
        for j, t in enumerate((kn, ukv[:, LANES:2 * LANES])):
            swapped = pltpu.roll(t, HEAD_DIM, 1)
            kv_ref[0, m_lo:m_hi, (2 * j) * LANES:(2 * j + 1) * LANES] = jnp.where(low_head, t, swapped).astype(BF16)
            kv_ref[0, m_lo:m_hi, (2 * j + 1) * LANES:(2 * j + 2) * LANES] = (
                jnp.where(low_head, swapped, t).astype(BF16))

    for p, (lo, hi) in enumerate(pieces):
        hb = normed(p)
        stage_short(p, hb)
        stage_glu(p, hb)
        q_outputs(p, hb, (0, D_ATTN // LANES))
        kv_outputs(p, hb)
        for c in range(tb // CONV_ROWS):
            if lo < (c + 1) * CONV_ROWS + 2 * HALO <= hi:
                conv_chunk(c)


def _proj(x, mods, mod_row, norm_g, w_in_b, layer, cos_t, sin_t, gq, gk, conv_a_w, conv_c_w, conv_c_b, ln_g, ln_b,
          tb):
    b, n, _ = x.shape
    const = lambda bb, i: (0, 0)
    hb = tb // HALO
    kern = functools.partial(_proj_kernel, tb, n)
    return pl.pallas_call(
        kern,
        grid=(b, n // tb),
        in_specs=[
            pl.BlockSpec((1, tb, D_MODEL), lambda bb, i: (bb, i, 0)),
            pl.BlockSpec((1, HALO, D_MODEL), lambda bb, i: (bb, jnp.maximum(i * hb - 1, 0), 0)),
            pl.BlockSpec((1, HALO, D_MODEL), lambda bb, i: (bb, jnp.minimum((i + 1) * hb, n // HALO - 1), 0)),
            pl.BlockSpec((1, 1, D_MODEL), lambda bb, i: (mod_row(bb), 0, 0)),
            pl.BlockSpec((1, 1, D_MODEL), lambda bb, i: (mod_row(bb), 0, 1)),
            pl.BlockSpec((1, D_MODEL), const),
            pl.BlockSpec((None, D_MODEL, D_IN), lambda bb, i: (layer, 0, 0)),
            pl.BlockSpec((tb, LANES), lambda bb, i: (i, 0)),
            pl.BlockSpec((tb, LANES), lambda bb, i: (i, 0)),
            pl.BlockSpec((1, LANES), const),
            pl.BlockSpec((1, LANES), const),
            pl.BlockSpec((SHORT_W, D_CONV), const),
            pl.BlockSpec((CONF_W, D_CONV), const),
            pl.BlockSpec((1, D_CONV), const),
            pl.BlockSpec((1, D_CONV), const),
            pl.BlockSpec((1, D_CONV), const),
        ],
        out_specs=[
            pl.BlockSpec((1, tb, 2 * D_CONV), lambda bb, i: (bb, i, 0)),
            pl.BlockSpec((1, tb, D_ATTN), lambda bb, i: (bb, i, 0)),
            pl.BlockSpec((1, tb, 4 * LANES), lambda bb, i: (bb, i, 0)),
        ],
        out_shape=[
            jax.ShapeDtypeStruct((b, n, 2 * D_CONV), BF16),
            jax.ShapeDtypeStruct((b, n, D_ATTN), BF16),
            jax.ShapeDtypeStruct((b, n, 4 * LANES), BF16),
        ],
        scratch_shapes=[pltpu.VMEM((tb + 2 * HALO, 3 * D_CONV), F32)],
        compiler_params=pltpu.CompilerParams(
            dimension_semantics=("parallel", "parallel"), vmem_limit_bytes=VMEM_LIMIT),
        name="proj",
    )(x, x, x, mods, mods, norm_g, w_in_b, cos_t, sin_t, gq, gk, conv_a_w, conv_c_w, conv_c_b, ln_g, ln_b)


def _mix_kernel(local, tb, seq,
                x_ref, y_ref, q_ref, kv_ref, kvp_ref, kvn_ref, ckv_ref, sink_ref, wo_ref, g1_ref,
                o_ref, kvbuf, ybuf):
    i = pl.program_id(1)
    lane = lax.broadcasted_iota(jnp.int32, (1, LANES), 1)
    low_head = lane < HEAD_DIM
    if local:
        kvbuf[0:WINDOW, :] = kvp_ref[0]
        kvbuf[WINDOW:WINDOW + tb, :] = kv_ref[0]
        kvbuf[WINDOW + tb:WINDOW + tb + WINDOW, :] = kvn_ref[0]
        n_loc = 3 * WINDOW
        rel = (lax.broadcasted_iota(jnp.int32, (WINDOW, WINDOW), 1)
               - lax.broadcasted_iota(jnp.int32, (WINDOW, WINDOW), 0))
    ones_cols = jnp.ones((1, LANES), BF16)

    def with_ones(v, keep_low):
        zero = jnp.zeros_like(v)
        masked = jnp.where(low_head, v, zero) if keep_low else jnp.where(low_head, zero, v)
        return jnp.concatenate([masked, jnp.broadcast_to(ones_cols, v.shape)], axis=1)

    def project(j):
        j0 = pl.multiple_of(j * WINDOW, WINDOW)
        out = _dot(ybuf[pl.ds(j0, WINDOW), :], wo_ref[...])
        o_ref[0, pl.ds(j0, WINDOW), :] = x_ref[0, pl.ds(j0, WINDOW), :] + g1_ref[0] * out

    def mixer_rows(j, prev=None):
        j0 = pl.multiple_of(j * WINDOW, WINDOW)
        yj = y_ref[0, pl.ds(j0, WINDOW), :]
        y_parts = [yj[:, 0:D_CONV]]

        qj = q_ref[0, pl.ds(j0, WINDOW), :]
        if local:
            base = i * tb + j0
            big = jnp.int32(1 << 20)
            prev_ok = rel >= jnp.where(base >= WINDOW, 0, big)
            next_ok = rel <= jnp.where(base + 2 * WINDOW <= seq, 0, -big)
        scores = []
        for kvh in range(N_KV):
            parts = []
            for g in range(GROUP):
                h = kvh * GROUP + g
                slab = qj[:, (h // 2) * LANES:(h // 2 + 1) * LANES]
                parts.append(jnp.where(low_head if h % 2 == 0 else jnp.logical_not(low_head),
                                       slab, jnp.zeros_like(slab)))
            qs = jnp.concatenate(parts, axis=0)
            kc = ckv_ref[0, :, kvh * LANES:(kvh + 1) * LANES]
            vc = ckv_ref[0, :, (N_KV + kvh) * LANES:(N_KV + kvh + 1) * LANES]
            s_ctx = _dot_nt(qs, kc)
            vcs = (with_ones(vc, True), with_ones(vc, False))
            s_loc = vws = None
            if local:
                kw = kvbuf[pl.ds(j0, n_loc), kvh * LANES:(kvh + 1) * LANES]
                vw = kvbuf[pl.ds(j0, n_loc), (N_KV + kvh) * LANES:(N_KV + kvh + 1) * LANES]
                s_loc = _dot_nt(qs, kw)
                vws = (with_ones(vw, True), with_ones(vw, False))
            scores.append((s_ctx, vcs, s_loc, vws))
            if kvh == 0 and prev is not None:
                project(prev)
        for kvh in range(N_KV):
            s_ctx, vcs, s_loc, vws = scores[kvh]
            for pair in range(GROUP // 2):
                num, den = [], []
                for half in range(2):
                    g = 2 * pair + half
                    sk = sink_ref[kvh * GROUP + g] * LOG2E
                    sc = s_ctx[g * WINDOW:(g + 1) * WINDOW]
                    if local:
                        sg = s_loc[g * WINDOW:(g + 1) * WINDOW]
                        sl = jnp.concatenate([
                            jnp.where(prev_ok, sg[:, 0:WINDOW], NEG_INF),
                            sg[:, WINDOW:2 * WINDOW],
                            jnp.where(next_ok, sg[:, 2 * WINDOW:3 * WINDOW], NEG_INF)], axis=1)
                        m = jnp.maximum(jnp.maximum(jnp.max(sl, axis=-1, keepdims=True),
                                                    jnp.max(sc, axis=-1, keepdims=True)), sk)
                        o = (_dot(jnp.exp2(sl - m).astype(BF16), vws[half])
                             + _dot(jnp.exp2(sc - m).astype(BF16), vcs[half]))
                    else:
                        m = jnp.maximum(jnp.max(sc, axis=-1, keepdims=True), sk)
                        o = _dot(jnp.exp2(sc - m).astype(BF16), vcs[half])
                    num.append(o[:, 0:LANES])
                    den.append(o[:, LANES:2 * LANES] + jnp.exp2(sk - m))
                o_pair = (num[0] + num[1]) / jnp.where(low_head, den[0], den[1])
                y_parts.append(o_pair.astype(BF16))
        y_parts.append(yj[:, D_CONV:2 * D_CONV])
        return jnp.concatenate(y_parts, axis=1)

    n_blocks = tb // WINDOW
    ybuf[0:WINDOW, :] = mixer_rows(0)

    def step(j, carry):
        ybuf[pl.ds(pl.multiple_of(j * WINDOW, WINDOW), WINDOW), :] = mixer_rows(j, j - 1)
        return carry

    lax.fori_loop(1, n_blocks, step, 0)
    project(n_blocks - 1)


def _mix(x, y, q, kv, ckv, mods, mod_row, sink, w_out_b, layer, tb, local):
    b, n, _ = x.shape
    n_ctx = ckv.shape[1]
    const = lambda bb, i: (0, 0)
    wb = tb // WINDOW
    kern = functools.partial(_mix_kernel, local, tb, n)
    return pl.pallas_call(
        kern,
        grid=(b, n // tb),
        in_specs=[
            pl.BlockSpec((1, tb, D_MODEL), lambda bb, i: (bb, i, 0)),
            pl.BlockSpec((1, tb, 2 * D_CONV), lambda bb, i: (bb, i, 0)),
            pl.BlockSpec((1, tb, D_ATTN), lambda bb, i: (bb, i, 0)),
            pl.BlockSpec((1, tb, 4 * LANES), lambda bb, i: (bb, i, 0)),
            pl.BlockSpec((1, WINDOW, 4 * LANES), lambda bb, i: (bb, jnp.maximum(i * wb - 1, 0), 0)),
            pl.BlockSpec((1, WINDOW, 4 * LANES),
                         lambda bb, i: (bb, jnp.minimum((i + 1) * wb, n // WINDOW - 1), 0)),
            pl.BlockSpec((1, n_ctx, 4 * LANES), lambda bb, i: (bb, 0, 0)),
            pl.BlockSpec(memory_space=pltpu.SMEM),
            pl.BlockSpec((None, D_MODEL, D_MODEL), lambda bb, i: (layer, 0, 0)),
            pl.BlockSpec((1, 1, D_MODEL), lambda bb, i: (mod_row(bb), 0, 2)),
        ],
        out_specs=pl.BlockSpec((1, tb, D_MODEL), lambda bb, i: (bb, i, 0)),
        out_shape=jax.ShapeDtypeStruct((b, n, D_MODEL), F32),
        scratch_shapes=[
            pltpu.VMEM((tb + 2 * WINDOW, 4 * LANES), BF16),
            pltpu.VMEM((tb, D_MODEL), BF16),
        ],
        compiler_params=pltpu.CompilerParams(
            dimension_semantics=("parallel", "parallel"), vmem_limit_bytes=VMEM_LIMIT),
        name="mix_local" if local else "mix_ctx",
    )(x, y, q, kv, kv, kv, ckv, sink, w_out_b, mods)


def _mlp_kernel(x_ref, sh_ref, sc_ref, gate_ref, g_ref, w1_ref, w2_ref, o_ref):
    x = x_ref[0]
    ms = jnp.mean(x * x, axis=-1, keepdims=True)
    h = (x * lax.rsqrt(ms + EPS)) * g_ref[...]
    h = h * (1.0 + sc_ref[0]) + sh_ref[0]
    hb = h.astype(BF16)
    acc = None
    for c in range(D_FF // FF_CHUNK):
        hid = _dot(hb, w1_ref[:, c * FF_CHUNK:(c + 1) * FF_CHUNK])
        r = jnp.maximum(hid, 0.0)
        part = _dot((r * r).astype(BF16), w2_ref[c * FF_CHUNK:(c + 1) * FF_CHUNK, :])
        acc = part if acc is None else acc + part
    o_ref[0] = x + gate_ref[0] * acc


def _mlp(x, mods, mod_row, norm_g, w1_b, w2_b, layer, tb):
    b, n, _ = x.shape
    const = lambda bb, i: (0, 0)
    return pl.pallas_call(
        _mlp_kernel,
        grid=(b, n // tb),
        in_specs=[
            pl.BlockSpec((1, tb, D_MODEL), lambda bb, i: (bb, i, 0)),
            pl.BlockSpec((1, 1, D_MODEL), lambda bb, i: (mod_row(bb), 0, 3)),
            pl.BlockSpec((1, 1, D_MODEL), lambda bb, i: (mod_row(bb), 0, 4)),
            pl.BlockSpec((1, 1, D_MODEL), lambda bb, i: (mod_row(bb), 0, 5)),
            pl.BlockSpec((1, D_MODEL), const),
            pl.BlockSpec((None, D_MODEL, D_FF), lambda bb, i: (layer, 0, 0)),
            pl.BlockSpec((None, D_FF, D_MODEL), lambda bb, i: (layer, 0, 0)),
        ],
        out_specs=pl.BlockSpec((1, tb, D_MODEL), lambda bb, i: (bb, i, 0)),
        out_shape=jax.ShapeDtypeStruct((b, n, D_MODEL), F32),
        compiler_params=pltpu.CompilerParams(
            dimension_semantics=("parallel", "parallel"), vmem_limit_bytes=VMEM_LIMIT),
        name="mlp",
    )(x, mods, mods, mods, norm_g, w1_b, w2_b)


def _rope_tables(seq):
    d_axis = HEAD_DIM // 2
    inv_freq = ROPE_BASE ** (-jnp.arange(0, d_axis, 2, dtype=F32) / d_axis)
    pos = jnp.arange(seq, dtype=jnp.int32)
    ang_r = (pos // GRID_W).astype(F32)[:, None] * inv_freq[None, :]
    ang_c = (pos % GRID_W).astype(F32)[:, None] * inv_freq[None, :]
    cos = jnp.concatenate([jnp.cos(ang_r), jnp.cos(ang_r), jnp.cos(ang_c), jnp.cos(ang_c)], axis=-1)
    sin = jnp.concatenate([-jnp.sin(ang_r), jnp.sin(ang_r), -jnp.sin(ang_c), jnp.sin(ang_c)], axis=-1)
    reps = LANES // HEAD_DIM
    return jnp.tile(cos, (1, reps)), jnp.tile(sin, (1, reps))


def kernel(x, c, ctx, c_ctx, w_mod, b_mod, norm1_g, w_in, conv_a_w, q_norm_g, k_norm_g, attn_sink,
           conv_c_w, conv_c_b, ln_c_g, ln_c_b, w_out, norm2_g, w_mlp1, w_mlp2):
    batch, seq, _ = x.shape
    n_ctx = ctx.shape[1]
    depth = w_mod.shape[0]
    tb = min(TOKEN_BLOCK, seq)
    tbw = min(WIDE_BLOCK, seq)
    tb_ctx = n_ctx
    ctx_row = batch

    cvec = jnp.concatenate([c, c_ctx[None, :], jnp.zeros((8 - batch - 1, D_MODEL), F32)], axis=0)
    mods_all = _modulation(cvec, w_mod, b_mod).reshape(depth, 8, 1, 6 * D_MODEL)

    cos_t, sin_t = _rope_tables(seq)
    cos_c = jnp.ones((n_ctx, LANES), F32)
    sin_c = jnp.zeros((n_ctx, LANES), F32)
    reps = LANES // HEAD_DIM
    lat_row = lambda bb: bb
    cx_row = lambda bb: ctx_row

    w_in_b = w_in.astype(BF16)
    w_out_b = w_out.astype(BF16)
    w1_b = w_mlp1.astype(BF16)
    w2_b = w_mlp2.astype(BF16)

    xc = ctx
    for l in range(depth):
        last = l == depth - 1
        mods = mods_all[l]
        g1n = norm1_g[l][None, :]
        g2n = norm2_g[l][None, :]
        gq = jnp.tile(q_norm_g[l], reps)[None, :]
        gk = jnp.tile(k_norm_g[l], reps)[None, :]
        conv_args = (conv_a_w[l], conv_c_w[l], conv_c_b[l][None, :], ln_c_g[l][None, :], ln_c_b[l][None, :])

        y_c, q_c, kv_c = _proj(xc, mods, cx_row, g1n, w_in_b, l, cos_c, sin_c, gq, gk, *conv_args, tb_ctx)
        y, q, kv = _proj(x, mods, lat_row, g1n, w_in_b, l, cos_t, sin_t, gq, gk, *conv_args, tbw)
        x = _mix(x, y, q, kv, kv_c, mods, lat_row, attn_sink[l], w_out_b, l, tbw, True)
        x = _mlp(x, mods, lat_row, g2n, w1_b, w2_b, l, tb)
        if not last:
            xc = _mix(xc, y_c, q_c, kv_c, kv_c, mods, cx_row, attn_sink[l], w_out_b, l, tb_ctx, False)
            xc = _mlp(xc, mods, cx_row, g2n, w1_b, w2_b, l, tb_ctx)
    return x
```

```python
import functools

import jax
import jax.numpy as jnp
from jax import lax
from jax.experimental import pallas as pl
from jax.experimental.pallas import tpu as pltpu

F32 = jnp.float32
BF16 = jnp.bfloat16

D_MODEL = 1024
HEAD_DIM = 64
N_HEADS = 8
N_KV = 2
GROUP = N_HEADS // N_KV
D_CONV = 256
D_ATTN = N_HEADS * HEAD_DIM
D_FF = 4 * D_MODEL
GRID_W = 64
WINDOW = 128
SHORT_W = 3
CONF_W = 31
HALO = 16
ROPE_BASE = 10000.0
EPS = 1e-6
NEG_INF = -1e30
LOG2E = 1.4426950408889634

OFF_Q = 3 * D_CONV
OFF_K = OFF_Q + D_ATTN
OFF_C = OFF_K + 2 * N_KV * HEAD_DIM
D_IN = OFF_C + 2 * D_CONV

LANES = 128
TOKEN_BLOCK = 512
WIDE_BLOCK = 1024
CONV_ROWS = 64
FF_CHUNK = 1024
MOD_COLS = 1536
VMEM_LIMIT = 56 * 1024 * 1024


def _sigmoid(t):
    return 1.0 / (1.0 + jnp.exp(-t))


def _dot(a, b):
    return jnp.dot(a, b, preferred_element_type=F32)


def _dot_nt(a, b):
    return lax.dot_general(a, b, (((1,), (1,)), ((), ())), preferred_element_type=F32)


def _mod_kernel(c_ref, w_ref, b_ref, o_ref):
    c = c_ref[...]
    s = c * _sigmoid(c)
    o_ref[0] = jnp.dot(s, w_ref[0], preferred_element_type=F32,
                       precision=lax.Precision.HIGHEST) + b_ref[0]


def _modulation(cvec, w_mod, b_mod):
    depth, _, n = w_mod.shape
    return pl.pallas_call(
        _mod_kernel,
        grid=(depth, n // MOD_COLS),
        in_specs=[
            pl.BlockSpec((8, D_MODEL), lambda l, j: (0, 0)),
            pl.BlockSpec((1, D_MODEL, MOD_COLS), lambda l, j: (l, 0, j)),
            pl.BlockSpec((1, 1, MOD_COLS), lambda l, j: (l, 0, j)),
        ],
        out_specs=pl.BlockSpec((1, 8, MOD_COLS), lambda l, j: (l, 0, j)),
        out_shape=jax.ShapeDtypeStruct((depth, 8, n), F32),
        compiler_params=pltpu.CompilerParams(
            dimension_semantics=("arbitrary", "arbitrary"), vmem_limit_bytes=VMEM_LIMIT),
        name="modulation",
    )(cvec, w_mod, b_mod.reshape(depth, 1, n))


def _phase_conv(buf, col0, w_ref, n_taps, r0, rows):
    phases = {}
    for k in range(n_taps):
        off = HALO + k - n_taps // 2
        phases.setdefault(off % 8, []).append((off - off % 8, k))
    out = None
    for p in sorted(phases):
        n_rows = rows if p == 0 else rows + 8
        z = None
        for a8, k in phases[p]:
            term = buf[r0 + a8:r0 + a8 + n_rows, col0:col0 + D_CONV] * w_ref[k:k + 1, :]
            z = term if z is None else z + term
        z = z[p:p + rows]
        out = z if out is None else out + z
    return out


def _proj_kernel(tb, seq,
                 x_ref, xp_ref, xn_ref, sh_ref, sc_ref, g_ref, w_ref, rot_row_ref, rot_col_ref, gq_ref, gk_ref,
                 wa_ref, wc_ref, cb_ref, lg_ref, lb_ref,
                 y_ref, q_ref, kv_ref, cbuf, cos_ref, sin_ref):
    i = pl.program_id(1)
    has_prev = i > 0
    has_next = i < seq // tb - 1

    by_row = (lax.broadcasted_iota(jnp.int32, (1, LANES), 1) % HEAD_DIM) < HEAD_DIM // 2
    for r in range(tb // GRID_W):
        cos_ref[r * GRID_W:(r + 1) * GRID_W, :] = jnp.where(by_row, rot_row_ref[0, r:r + 1, :], rot_col_ref[0])
        sin_ref[r * GRID_W:(r + 1) * GRID_W, :] = jnp.where(by_row, rot_row_ref[1, r:r + 1, :], rot_col_ref[1])

    def norm_mod(x):
        ms = jnp.mean(x * x, axis=-1, keepdims=True)
        h = (x * lax.rsqrt(ms + EPS)) * g_ref[...]
        return (h * (1.0 + sc_ref[0]) + sh_ref[0]).astype(BF16)

    lane = lax.broadcasted_iota(jnp.int32, (1, LANES), 1)
    rot_from_above = (lane % (HEAD_DIM // 2)) < (HEAD_DIM // 4)
    low_head = lane < HEAD_DIM
    r_i = lax.broadcasted_iota(jnp.int32, (LANES, LANES), 0) // HEAD_DIM
    c_i = lax.broadcasted_iota(jnp.int32, (LANES, LANES), 1) // HEAD_DIM
    head_ones = jnp.where(r_i == c_i, 1.0, 0.0).astype(BF16)

    def norm_rope(t, gain, scale, cos, sin):
        ssq = _dot((t * t).astype(BF16), head_ones)
        rs = lax.rsqrt(ssq * (1.0 / HEAD_DIM) + EPS) * scale
        tg = t * gain
        partner = jnp.where(rot_from_above, pltpu.roll(tg, LANES - HEAD_DIM // 4, 1),
                            pltpu.roll(tg, HEAD_DIM // 4, 1))
        return rs * (tg * cos + partner * sin)

    def conv_chunk(c):
        r0 = c * CONV_ROWS
        a = _phase_conv(cbuf, 0, wa_ref, SHORT_W, r0, CONV_ROWS)
        b_gate = cbuf[HALO + r0:HALO + r0 + CONV_ROWS, 2 * D_CONV:3 * D_CONV]
        y_ref[0, r0:r0 + CONV_ROWS, 0:D_CONV] = (b_gate * a).astype(BF16)
        acc = _phase_conv(cbuf, D_CONV, wc_ref, CONF_W, r0, CONV_ROWS) + cb_ref[...]
        mu = jnp.mean(acc, axis=-1, keepdims=True)
        xc = acc - mu
        var = jnp.mean(xc * xc, axis=-1, keepdims=True)
        yn = (xc * lax.rsqrt(var + EPS)) * lg_ref[...] + lb_ref[...]
        y_ref[0, r0:r0 + CONV_ROWS, D_CONV:2 * D_CONV] = (yn * _sigmoid(yn)).astype(BF16)

    half = tb // 2
    pieces = ((0, half + 2 * HALO), (half + 2 * HALO, tb + 2 * HALO))
    main = tuple((max(lo - HALO, 0), min(hi - HALO, tb)) for lo, hi in pieces)

    def normed(p):
        lo, hi = pieces[p]
        parts = [norm_mod(x_ref[0, main[p][0]:main[p][1], :])]
        if lo == 0:
            parts.insert(0, norm_mod(xp_ref[0]))
        if hi == tb + 2 * HALO:
            parts.append(norm_mod(xn_ref[0]))
        return jnp.concatenate(parts, axis=0)

    def main_rows(p, hb):
        return hb[main[p][0] + HALO - pieces[p][0]:main[p][1] + HALO - pieces[p][0]]

    def stage_short(p, hb):
        lo, hi = pieces[p]
        ua = _dot(hb, w_ref[:, 0:OFF_Q])
        cbuf[lo:hi, 0:D_CONV] = ua[:, 2 * D_CONV:3 * D_CONV] * ua[:, 0:D_CONV]
        cbuf[lo:hi, 2 * D_CONV:3 * D_CONV] = ua[:, D_CONV:2 * D_CONV]

    def stage_glu(p, hb):
        lo, hi = pieces[p]
        uc = _dot(hb, w_ref[:, OFF_C:D_IN])
        cbuf[lo:hi, D_CONV:2 * D_CONV] = uc[:, 0:D_CONV] * _sigmoid(uc[:, D_CONV:2 * D_CONV])
        if lo == 0:
            cbuf[0:HALO, 0:2 * D_CONV] = jnp.where(has_prev, cbuf[0:HALO, 0:2 * D_CONV], 0.0)
        if hi == tb + 2 * HALO:
            cbuf[hi - HALO:hi, 0:2 * D_CONV] = jnp.where(has_next, cbuf[hi - HALO:hi, 0:2 * D_CONV], 0.0)

    def q_outputs(p, hb, slabs):
        m_lo, m_hi = main[p]
        s0, s1 = slabs
        uq = _dot(main_rows(p, hb), w_ref[:, OFF_Q + s0 * LANES:OFF_Q + s1 * LANES])
        for s in range(s0, s1):
            qs = norm_rope(uq[:, (s - s0) * LANES:(s - s0 + 1) * LANES], gq_ref[...], LOG2E * HEAD_DIM ** -0.5,
                           cos_ref[m_lo:m_hi, :], sin_ref[m_lo:m_hi, :])
            q_ref[0, m_lo:m_hi, s * LANES:(s + 1) * LANES] = qs.astype(BF16)

    def kv_outputs(p, hb):
        m_lo, m_hi = main[p]
        ukv = _dot(main_rows(p, hb), w_ref[:, OFF_K:OFF_C])
        kn = norm_rope(ukv[:, 0:LANES], gk_ref[...], 1.0, cos_ref[m_lo:m_hi, :], sin_ref[m_lo:m_hi, :])
        for j, t in enumerate((kn, ukv[:, LANES:2 * LANES])):
            swapped = pltpu.roll(t, HEAD_DIM, 1)
            kv_ref[0, m_lo:m_hi, (2 * j) * LANES:(2 * j + 1) * LANES] = jnp.where(low_head, t, swapped).astype(BF16)
            kv_ref[0, m_lo:m_hi, (2 * j + 1) * LANES:(2 * j + 2) * LANES] = (
                jnp.where(low_head, swapped, t).astype(BF16))

    for p, (lo, hi) in enumerate(pieces):
        hb = normed(p)
        stage_short(p, hb)
        stage_glu(p, hb)
        q_outputs(p, hb, (0, D_ATTN // LANES))
        kv_outputs(p, hb)
        for c in range(tb // CONV_ROWS):
            if lo < (c + 1) * CONV_ROWS + 2 * HALO <= hi:
                conv_chunk(c)


def _proj(x, mods, mod_row, norm_g, w_in_b, layer, rot_row, rot_col, gq, gk, conv_a_w, conv_c_w, conv_c_b, ln_g,
          ln_b, tb):
    b, n, _ = x.shape
    const = lambda bb, i: (0, 0)
    hb = tb // HALO
    kern = functools.partial(_proj_kernel, tb, n)
    return pl.pallas_call(
        kern,
        grid=(b, n // tb),
        in_specs=[
            pl.BlockSpec((1, tb, D_MODEL), lambda bb, i: (bb, i, 0)),
            pl.BlockSpec((1, HALO, D_MODEL), lambda bb, i: (bb, jnp.maximum(i * hb - 1, 0), 0)),
            pl.BlockSpec((1, HALO, D_MODEL), lambda bb, i: (bb, jnp.minimum((i + 1) * hb, n // HALO - 1), 0)),
            pl.BlockSpec((1, 1, D_MODEL), lambda bb, i: (mod_row(bb), 0, 0)),
            pl.BlockSpec((1, 1, D_MODEL), lambda bb, i: (mod_row(bb), 0, 1)),
            pl.BlockSpec((1, D_MODEL), const),
            pl.BlockSpec((None, D_MODEL, D_IN), lambda bb, i: (layer, 0, 0)),
            pl.BlockSpec((2, tb // GRID_W, LANES), lambda bb, i: (0, i, 0)),
            pl.BlockSpec((2, GRID_W, LANES), lambda bb, i: (0, 0, 0)),
            pl.BlockSpec((1, LANES), const),
            pl.BlockSpec((1, LANES), const),
            pl.BlockSpec((SHORT_W, D_CONV), const),
            pl.BlockSpec((CONF_W, D_CONV), const),
            pl.BlockSpec((1, D_CONV), const),
            pl.BlockSpec((1, D_CONV), const),
            pl.BlockSpec((1, D_CONV), const),
        ],
        out_specs=[
            pl.BlockSpec((1, tb, 2 * D_CONV), lambda bb, i: (bb, i, 0)),
            pl.BlockSpec((1, tb, D_ATTN), lambda bb, i: (bb, i, 0)),
            pl.BlockSpec((1, tb, 4 * LANES), lambda bb, i: (bb, i, 0)),
        ],
        out_shape=[
            jax.ShapeDtypeStruct((b, n, 2 * D_CONV), BF16),
            jax.ShapeDtypeStruct((b, n, D_ATTN), BF16),
            jax.ShapeDtypeStruct((b, n, 4 * LANES), BF16),
        ],
        scratch_shapes=[pltpu.VMEM((tb + 2 * HALO, 3 * D_CONV), F32),
                        pltpu.VMEM((tb, LANES), F32), pltpu.VMEM((tb, LANES), F32)],
        compiler_params=pltpu.CompilerParams(
            dimension_semantics=("parallel", "parallel"), vmem_limit_bytes=VMEM_LIMIT),
        name="proj",
    )(x, x, x, mods, mods, norm_g, w_in_b, rot_row, rot_col, gq, gk, conv_a_w, conv_c_w, conv_c_b, ln_g, ln_b)


def _mix_kernel(local, tb, seq,
                x_ref, y_ref, q_ref, kv_ref, kvp_ref, kvn_ref, ckv_ref, sink_ref, wo_ref, g1_ref,
                o_ref, kvbuf, ybuf):
    i = pl.program_id(1)
    lane = lax.broadcasted_iota(jnp.int32, (1, LANES), 1)
    low_head = lane < HEAD_DIM
    if local:
        kvbuf[0:WINDOW, :] = kvp_ref[0]
        kvbuf[WINDOW:WINDOW + tb, :] = kv_ref[0]
        kvbuf[WINDOW + tb:WINDOW + tb + WINDOW, :] = kvn_ref[0]
        n_loc = 3 * WINDOW
        rel = (lax.broadcasted_iota(jnp.int32, (WINDOW, WINDOW), 1)
               - lax.broadcasted_iota(jnp.int32, (WINDOW, WINDOW), 0))
    ones_cols = jnp.ones((1, LANES), BF16)

    def with_ones(v, keep_low):
        zero = jnp.zeros_like(v)
        masked = jnp.where(low_head, v, zero) if keep_low else jnp.where(low_head, zero, v)
        return jnp.concatenate([masked, jnp.broadcast_to(ones_cols, v.shape)], axis=1)

    def project(j):
        j0 = pl.multiple_of(j * WINDOW, WINDOW)
        out = _dot(ybuf[pl.ds(j0, WINDOW), :], wo_ref[...])
        o_ref[0, pl.ds(j0, WINDOW), :] = x_ref[0, pl.ds(j0, WINDOW), :] + g1_ref[0] * out

    def mixer_rows(j, prev=None):
        j0 = pl.multiple_of(j * WINDOW, WINDOW)
        yj = y_ref[0, pl.ds(j0, WINDOW), :]
        y_parts = [yj[:, 0:D_CONV]]

        qj = q_ref[0, pl.ds(j0, WINDOW), :]
        if local:
            base = i * tb + j0
            big = jnp.int32(1 << 20)
            prev_ok = rel >= jnp.where(base >= WINDOW, 0, big)
            next_ok = rel <= jnp.where(base + 2 * WINDOW <= seq, 0, -big)
        scores = []
        for kvh in range(N_KV):
            parts = []
            for g in range(GROUP):
                h = kvh * GROUP + g
                slab = qj[:, (h // 2) * LANES:(h // 2 + 1) * LANES]
                parts.append(jnp.where(low_head if h % 2 == 0 else jnp.logical_not(low_head),
                                       slab, jnp.zeros_like(slab)))
            qs = jnp.concatenate(parts, axis=0)
            kc = ckv_ref[0, :, kvh * LANES:(kvh + 1) * LANES]
            vc = ckv_ref[0, :, (N_KV + kvh) * LANES:(N_KV + kvh + 1) * LANES]
            s_ctx = _dot_nt(qs, kc)
            vcs = (with_ones(vc, True), with_ones(vc, False))
            s_loc = vws = None
            if local:
                kw = kvbuf[pl.ds(j0, n_loc), kvh * LANES:(kvh + 1) * LANES]
                vw = kvbuf[pl.ds(j0, n_loc), (N_KV + kvh) * LANES:(N_KV + kvh + 1) * LANES]
                s_loc = _dot_nt(qs, kw)
                vws = (with_ones(vw, True), with_ones(vw, False))
            scores.append((s_ctx, vcs, s_loc, vws))
            if kvh == 0 and prev is not None:
                project(prev)
        for kvh in range(N_KV):
            s_ctx, vcs, s_loc, vws = scores[kvh]
            for pair in range(GROUP // 2):
                num, den = [], []
                for half in range(2):
                    g = 2 * pair + half
                    sk = sink_ref[kvh * GROUP + g] * LOG2E
                    sc = s_ctx[g * WINDOW:(g + 1) * WINDOW]
                    if local:
                        sg = s_loc[g * WINDOW:(g + 1) * WINDOW]
                        sl = jnp.concatenate([
                            jnp.where(prev_ok, sg[:, 0:WINDOW], NEG_INF),
                            sg[:, WINDOW:2 * WINDOW],
                            jnp.where(next_ok, sg[:, 2 * WINDOW:3 * WINDOW], NEG_INF)], axis=1)
                        m = jnp.maximum(jnp.maximum(jnp.max(sl, axis=-1, keepdims=True),
                                                    jnp.max(sc, axis=-1, keepdims=True)), sk)
                        o = (_dot(jnp.exp2(sl - m).astype(BF16), vws[half])
                             + _dot(jnp.exp2(sc - m).astype(BF16), vcs[half]))
                    else:
                        m = jnp.maximum(jnp.max(sc, axis=-1, keepdims=True), sk)
                        o = _dot(jnp.exp2(sc - m).astype(BF16), vcs[half])
                    num.append(o[:, 0:LANES])
                    den.append(o[:, LANES:2 * LANES] + jnp.exp2(sk - m))
                o_pair = (num[0] + num[1]) / jnp.where(low_head, den[0], den[1])
                y_parts.append(o_pair.astype(BF16))
        y_parts.append(yj[:, D_CONV:2 * D_CONV])
        return jnp.concatenate(y_parts, axis=1)

    n_blocks = tb // WINDOW
    ybuf[0:WINDOW, :] = mixer_rows(0)

    for j in range(1, n_blocks):
        ybuf[j * WINDOW:(j + 1) * WINDOW, :] = mixer_rows(j, j - 1)
    project(n_blocks - 1)


def _mix(x, y, q, kv, ckv, mods, mod_row, sink, w_out_b, layer, tb, local):
    b, n, _ = x.shape
    n_ctx = ckv.shape[1]
    const = lambda bb, i: (0, 0)
    wb = tb // WINDOW
    kern = functools.partial(_mix_kernel, local, tb, n)
    return pl.pallas_call(
        kern,
        grid=(b, n // tb),
        in_specs=[
            pl.BlockSpec((1, tb, D_MODEL), lambda bb, i: (bb, i, 0)),
            pl.BlockSpec((1, tb, 2 * D_CONV), lambda bb, i: (bb, i, 0)),
            pl.BlockSpec((1, tb, D_ATTN), lambda bb, i: (bb, i, 0)),
            pl.BlockSpec((1, tb, 4 * LANES), lambda bb, i: (bb, i, 0)),
            pl.BlockSpec((1, WINDOW, 4 * LANES), lambda bb, i: (bb, jnp.maximum(i * wb - 1, 0), 0)),
            pl.BlockSpec((1, WINDOW, 4 * LANES),
                         lambda bb, i: (bb, jnp.minimum((i + 1) * wb, n // WINDOW - 1), 0)),
            pl.BlockSpec((1, n_ctx, 4 * LANES), lambda bb, i: (bb, 0, 0)),
            pl.BlockSpec(memory_space=pltpu.SMEM),
            pl.BlockSpec((None, D_MODEL, D_MODEL), lambda bb, i: (layer, 0, 0)),
            pl.BlockSpec((1, 1, D_MODEL), lambda bb, i: (mod_row(bb), 0, 2)),
        ],
        out_specs=pl.BlockSpec((1, tb, D_MODEL), lambda bb, i: (bb, i, 0)),
        out_shape=jax.ShapeDtypeStruct((b, n, D_MODEL), F32),
        scratch_shapes=[
            pltpu.VMEM((tb + 2 * WINDOW, 4 * LANES), BF16),
            pltpu.VMEM((tb, D_MODEL), BF16),
        ],
        compiler_params=pltpu.CompilerParams(
            dimension_semantics=("parallel", "parallel"), vmem_limit_bytes=VMEM_LIMIT),
        name="mix_local" if local else "mix_ctx",
    )(x, y, q, kv, kv, kv, ckv, sink, w_out_b, mods)


def _mlp_kernel(x_ref, sh_ref, sc_ref, gate_ref, g_ref, w1_ref, w2_ref, o_ref):
    x = x_ref[0]
    ms = jnp.mean(x * x, axis=-1, keepdims=True)
    h = (x * lax.rsqrt(ms + EPS)) * g_ref[...]
    h = h * (1.0 + sc_ref[0]) + sh_ref[0]
    hb = h.astype(BF16)
    acc = None
    for c in range(D_FF // FF_CHUNK):
        hid = _dot(hb, w1_ref[:, c * FF_CHUNK:(c + 1) * FF_CHUNK])
        r = jnp.maximum(hid, 0.0)
        part = _dot((r * r).astype(BF16), w2_ref[c * FF_CHUNK:(c + 1) * FF_CHUNK, :])
        acc = part if acc is None else acc + part
    o_ref[0] = x + gate_ref[0] * acc


def _mlp(x, mods, mod_row, norm_g, w1_b, w2_b, layer, tb):
    b, n, _ = x.shape
    const = lambda bb, i: (0, 0)
    return pl.pallas_call(
        _mlp_kernel,
        grid=(b, n // tb),
        in_specs=[
            pl.BlockSpec((1, tb, D_MODEL), lambda bb, i: (bb, i, 0)),
            pl.BlockSpec((1, 1, D_MODEL), lambda bb, i: (mod_row(bb), 0, 3)),
            pl.BlockSpec((1, 1, D_MODEL), lambda bb, i: (mod_row(bb), 0, 4)),
            pl.BlockSpec((1, 1, D_MODEL), lambda bb, i: (mod_row(bb), 0, 5)),
            pl.BlockSpec((1, D_MODEL), const),
            pl.BlockSpec((None, D_MODEL, D_FF), lambda bb, i: (layer, 0, 0)),
            pl.BlockSpec((None, D_FF, D_MODEL), lambda bb, i: (layer, 0, 0)),
        ],
        out_specs=pl.BlockSpec((1, tb, D_MODEL), lambda bb, i: (bb, i, 0)),
        out_shape=jax.ShapeDtypeStruct((b, n, D_MODEL), F32),
        compiler_params=pltpu.CompilerParams(
            dimension_semantics=("parallel", "parallel"), vmem_limit_bytes=VMEM_LIMIT),
        name="mlp",
    )(x, mods, mods, mods, norm_g, w1_b, w2_b)


def _rope_tables(seq):
    d_axis = HEAD_DIM // 2
    inv_freq = ROPE_BASE ** (-jnp.arange(0, d_axis, 2, dtype=F32) / d_axis)
    reps = LANES // HEAD_DIM

    def table(n_pos, first):
        ang = jnp.arange(n_pos, dtype=jnp.int32).astype(F32)[:, None] * inv_freq[None, :]
        zero = jnp.zeros_like(ang)
        cos = [jnp.cos(ang), jnp.cos(ang), zero, zero]
        sin = [-jnp.sin(ang), jnp.sin(ang), zero, zero]
        if not first:
            cos, sin = cos[2:] + cos[:2], sin[2:] + sin[:2]
        return jnp.stack([jnp.tile(jnp.concatenate(cos, axis=-1), (1, reps)),
                          jnp.tile(jnp.concatenate(sin, axis=-1), (1, reps))])

    return table(seq // GRID_W, True), table(GRID_W, False)


def kernel(x, c, ctx, c_ctx, w_mod, b_mod, norm1_g, w_in, conv_a_w, q_norm_g, k_norm_g, attn_sink,
           conv_c_w, conv_c_b, ln_c_g, ln_c_b, w_out, norm2_g, w_mlp1, w_mlp2):
    batch, seq, _ = x.shape
    n_ctx = ctx.shape[1]
    depth = w_mod.shape[0]
    tb = min(TOKEN_BLOCK, seq)
    tbw = min(WIDE_BLOCK, seq)
    tb_ctx = n_ctx
    ctx_row = batch

    cvec = jnp.concatenate([c, c_ctx[None, :], jnp.zeros((8 - batch - 1, D_MODEL), F32)], axis=0)
    mods_all = _modulation(cvec, w_mod, b_mod).reshape(depth, 8, 1, 6 * D_MODEL)

    rot_row, rot_col = _rope_tables(seq)
    rot_row_c = jnp.stack([jnp.ones((n_ctx // GRID_W, LANES), F32), jnp.zeros((n_ctx // GRID_W, LANES), F32)])
    rot_col_c = jnp.stack([jnp.ones((GRID_W, LANES), F32), jnp.zeros((GRID_W, LANES), F32)])
    reps = LANES // HEAD_DIM
    lat_row = lambda bb: bb
    cx_row = lambda bb: ctx_row

    w_in_b = w_in.astype(BF16)
    w_out_b = w_out.astype(BF16)
    w1_b = w_mlp1.astype(BF16)
    w2_b = w_mlp2.astype(BF16)

    xc = ctx
    for l in range(depth):
        last = l == depth - 1
        mods = mods_all[l]
        g1n = norm1_g[l][None, :]
        g2n = norm2_g[l][None, :]
        gq = jnp.tile(q_norm_g[l], reps)[None, :]
        gk = jnp.tile(k_norm_g[l], reps)[None, :]
        conv_args = (conv_a_w[l], conv_c_w[l], conv_c_b[l][None, :], ln_c_g[l][None, :], ln_c_b[l][None, :])

        y_c, q_c, kv_c = _proj(xc, mods, cx_row, g1n, w_in_b, l, rot_row_c, rot_col_c, gq, gk, *conv_args, tb_ctx)
        y, q, kv = _proj(x, mods, lat_row, g1n, w_in_b, l, rot_row, rot_col, gq, gk, *conv_args, tbw)
        x = _mix(x, y, q, kv, kv_c, mods, lat_row, attn_sink[l], w_out_b, l, tbw, True)
        x = _mlp(x, mods, lat_row, g2n, w1_b, w2_b, l, tb)
        if not last:
            xc = _mix(xc, y_c, q_c, kv_c, kv_c, mods, cx_row, attn_sink[l], w_out_b, l, tb_ctx, False)
            xc = _mlp(xc, mods, cx_row, g2n, w1_b, w2_b, l, tb_ctx)
    return x
```

```python
import functools

import jax
import jax.numpy as jnp
from jax import lax
from jax.experimental import pallas as pl
from jax.experimental.pallas import tpu as pltpu

F32 = jnp.float32
BF16 = jnp.bfloat16

D_MODEL = 1024
HEAD_DIM = 64
N_HEADS = 8
N_KV = 2
GROUP = N_HEADS // N_KV
D_CONV = 256
D_ATTN = N_HEADS * HEAD_DIM
D_FF = 4 * D_MODEL
GRID_W = 64
WINDOW = 128
SHORT_W = 3
CONF_W = 31
HALO = 16
ROPE_BASE = 10000.0
EPS = 1e-6
NEG_INF = -1e30
LOG2E = 1.4426950408889634

OFF_Q = 3 * D_CONV
OFF_K = OFF_Q + D_ATTN
OFF_C = OFF_K + 2 * N_KV * HEAD_DIM
D_IN = OFF_C + 2 * D_CONV

LANES = 128
TOKEN_BLOCK = 512
WIDE_BLOCK = 1024
CONV_ROWS = 64
FF_CHUNK = 1024
MOD_COLS = 1536
VMEM_LIMIT = 56 * 1024 * 1024


def _sigmoid(t):
    return 1.0 / (1.0 + jnp.exp(-t))


def _dot(a, b):
    return jnp.dot(a, b, preferred_element_type=F32)


def _dot_nt(a, b):
    return lax.dot_general(a, b, (((1,), (1,)), ((), ())), preferred_element_type=F32)


def _mod_kernel(n_vec, c_ref, w_ref, b_ref, o_ref):
    c = c_ref[...]
    s = c * _sigmoid(c)
    w = w_ref[0]
    rows = [jnp.sum(w * s[:, r:r + 1], axis=0, keepdims=True) for r in range(n_vec)]
    rows.append(jnp.zeros((8 - n_vec, w.shape[1]), F32))
    o_ref[0] = jnp.concatenate(rows, axis=0) + b_ref[0]


def _modulation(cvec_t, n_vec, w_mod, b_mod):
    depth, _, n = w_mod.shape
    return pl.pallas_call(
        functools.partial(_mod_kernel, n_vec),
        grid=(depth, n // MOD_COLS),
        in_specs=[
            pl.BlockSpec((D_MODEL, 8), lambda l, j: (0, 0)),
            pl.BlockSpec((1, D_MODEL, MOD_COLS), lambda l, j: (l, 0, j)),
            pl.BlockSpec((1, 1, MOD_COLS), lambda l, j: (l, 0, j)),
        ],
        out_specs=pl.BlockSpec((1, 8, MOD_COLS), lambda l, j: (l, 0, j)),
        out_shape=jax.ShapeDtypeStruct((depth, 8, n), F32),
        compiler_params=pltpu.CompilerParams(
            dimension_semantics=("arbitrary", "arbitrary"), vmem_limit_bytes=VMEM_LIMIT),
        name="modulation",
    )(cvec_t, w_mod, b_mod.reshape(depth, 1, n))


def _phase_conv(buf, col0, w_ref, n_taps, r0, rows):
    phases = {}
    for k in range(n_taps):
        off = HALO + k - n_taps // 2
        phases.setdefault(off % 8, []).append((off - off % 8, k))
    out = None
    for p in sorted(phases):
        n_rows = rows if p == 0 else rows + 8
        z = None
        for a8, k in phases[p]:
            xs = buf[r0 + a8:r0 + a8 + n_rows, col0:col0 + D_CONV]
            term = (xs.reshape(n_rows // 8, 8, D_CONV) * w_ref[k]).reshape(n_rows, D_CONV)
            z = term if z is None else z + term
        z = z[p:p + rows]
        out = z if out is None else out + z
    return out


def _proj_kernel(tb, seq,
                 x_ref, xp_ref, xn_ref, sh_ref, sc_ref, g_ref, w_ref, rot_row_ref, rot_col_ref, gq_ref, gk_ref,
                 wa_ref, wc_ref, cb_ref, lg_ref, lb_ref,
                 y_ref, q_ref, kv_ref, cbuf, cos_ref, sin_ref):
    i = pl.program_id(1)
    has_prev = i > 0
    has_next = i < seq // tb - 1

    by_row = (lax.broadcasted_iota(jnp.int32, (1, LANES), 1) % HEAD_DIM) < HEAD_DIM // 2
    for r in range(tb // GRID_W):
        cos_ref[r * GRID_W:(r + 1) * GRID_W, :] = jnp.where(by_row, rot_row_ref[0, r:r + 1, :], rot_col_ref[0])
        sin_ref[r * GRID_W:(r + 1) * GRID_W, :] = jnp.where(by_row, rot_row_ref[1, r:r + 1, :], rot_col_ref[1])

    gain = g_ref[...] * (1.0 + sc_ref[0])

    def norm_mod(x):
        ms = jnp.mean(x * x, axis=-1, keepdims=True)
        return ((x * lax.rsqrt(ms + EPS)) * gain + sh_ref[0]).astype(BF16)

    lane = lax.broadcasted_iota(jnp.int32, (1, LANES), 1)
    rot_from_above = (lane % (HEAD_DIM // 2)) < (HEAD_DIM // 4)
    low_head = lane < HEAD_DIM
    r_i = lax.broadcasted_iota(jnp.int32, (LANES, LANES), 0) // HEAD_DIM
    c_i = lax.broadcasted_iota(jnp.int32, (LANES, LANES), 1) // HEAD_DIM
    head_ones = jnp.where(r_i == c_i, 1.0, 0.0).astype(BF16)

    def norm_rope(t, gain, scale, cos, sin):
        ssq = _dot((t * t).astype(BF16), head_ones)
        rs = lax.rsqrt(ssq * (1.0 / HEAD_DIM) + EPS) * scale
        tg = t * gain
        partner = jnp.where(rot_from_above, pltpu.roll(tg, LANES - HEAD_DIM // 4, 1),
                            pltpu.roll(tg, HEAD_DIM // 4, 1))
        return rs * (tg * cos + partner * sin)

    def conv_chunk(c):
        r0 = c * CONV_ROWS
        a = _phase_conv(cbuf, 0, wa_ref, SHORT_W, r0, CONV_ROWS)
        b_gate = cbuf[HALO + r0:HALO + r0 + CONV_ROWS, 2 * D_CONV:3 * D_CONV]
        y_ref[0, r0:r0 + CONV_ROWS, 0:D_CONV] = (b_gate * a).astype(BF16)
        acc = _phase_conv(cbuf, D_CONV, wc_ref, CONF_W, r0, CONV_ROWS) + cb_ref[...]
        mu = jnp.mean(acc, axis=-1, keepdims=True)
        xc = acc - mu
        var = jnp.mean(xc * xc, axis=-1, keepdims=True)
        yn = (xc * lax.rsqrt(var + EPS)) * lg_ref[...] + lb_ref[...]
        y_ref[0, r0:r0 + CONV_ROWS, D_CONV:2 * D_CONV] = (yn * _sigmoid(yn)).astype(BF16)

    half = tb // 2
    pieces = ((0, half + 2 * HALO), (half + 2 * HALO, tb + 2 * HALO))
    main = tuple((max(lo - HALO, 0), min(hi - HALO, tb)) for lo, hi in pieces)

    def normed(p):
        lo, hi = pieces[p]
        parts = [norm_mod(x_ref[0, main[p][0]:main[p][1], :])]
        if lo == 0:
            parts.insert(0, norm_mod(xp_ref[0]))
        if hi == tb + 2 * HALO:
            parts.append(norm_mod(xn_ref[0]))
        return jnp.concatenate(parts, axis=0)

    def main_rows(p, hb):
        return hb[main[p][0] + HALO - pieces[p][0]:main[p][1] + HALO - pieces[p][0]]

    def stage_short(p, hb):
        lo, hi = pieces[p]
        ua = _dot(hb, w_ref[:, 0:OFF_Q])
        cbuf[lo:hi, 0:D_CONV] = ua[:, 2 * D_CONV:3 * D_CONV] * ua[:, 0:D_CONV]
        cbuf[lo:hi, 2 * D_CONV:3 * D_CONV] = ua[:, D_CONV:2 * D_CONV]

    def stage_glu(p, hb):
        lo, hi = pieces[p]
        uc = _dot(hb, w_ref[:, OFF_C:D_IN])
        cbuf[lo:hi, D_CONV:2 * D_CONV] = uc[:, 0:D_CONV] * _sigmoid(uc[:, D_CONV:2 * D_CONV])
        if lo == 0:
            cbuf[0:HALO, 0:2 * D_CONV] = jnp.where(has_prev, cbuf[0:HALO, 0:2 * D_CONV], 0.0)
        if hi == tb + 2 * HALO:
            cbuf[hi - HALO:hi, 0:2 * D_CONV] = jnp.where(has_next, cbuf[hi - HALO:hi, 0:2 * D_CONV], 0.0)

    def q_outputs(p, hb, slabs):
        m_lo, m_hi = main[p]
        s0, s1 = slabs
        uq = _dot(main_rows(p, hb), w_ref[:, OFF_Q + s0 * LANES:OFF_Q + s1 * LANES])
        for s in range(s0, s1):
            qs = norm_rope(uq[:, (s - s0) * LANES:(s - s0 + 1) * LANES], gq_ref[...], LOG2E * HEAD_DIM ** -0.5,
                           cos_ref[m_lo:m_hi, :], sin_ref[m_lo:m_hi, :])
            q_ref[0, m_lo:m_hi, s * LANES:(s + 1) * LANES] = qs.astype(BF16)

    def kv_outputs(p, hb):
        m_lo, m_hi = main[p]
        ukv = _dot(main_rows(p, hb), w_ref[:, OFF_K:OFF_C])
        kn = norm_rope(ukv[:, 0:LANES], gk_ref[...], 1.0, cos_ref[m_lo:m_hi, :], sin_ref[m_lo:m_hi, :])
        for j, t in enumerate((kn, ukv[:, LANES:2 * LANES])):
            swapped = pltpu.roll(t, HEAD_DIM, 1)
            kv_ref[0, m_lo:m_hi, (2 * j) * LANES:(2 * j + 1) * LANES] = jnp.where(low_head, t, swapped).astype(BF16)
            kv_ref[0, m_lo:m_hi, (2 * j + 1) * LANES:(2 * j + 2) * LANES] = (
                jnp.where(low_head, swapped, t).astype(BF16))

    for p, (lo, hi) in enumerate(pieces):
        hb = normed(p)
        stage_short(p, hb)
        stage_glu(p, hb)
        q_outputs(p, hb, (0, D_ATTN // LANES))
        kv_outputs(p, hb)
        for c in range(tb // CONV_ROWS):
            if lo < (c + 1) * CONV_ROWS + 2 * HALO <= hi:
                conv_chunk(c)


def _proj(x, mods, mod_row, norm_g, w_in_b, layer, rot_row, rot_col, gq, gk, conv_a_w, conv_c_w, conv_c_b, ln_g,
          ln_b, tb):
    b, n, _ = x.shape
    const = lambda bb, i: (0, 0)
    hb = tb // HALO
    kern = functools.partial(_proj_kernel, tb, n)
    return pl.pallas_call(
        kern,
        grid=(b, n // tb),
        in_specs=[
            pl.BlockSpec((1, tb, D_MODEL), lambda bb, i: (bb, i, 0)),
            pl.BlockSpec((1, HALO, D_MODEL), lambda bb, i: (bb, jnp.maximum(i * hb - 1, 0), 0)),
            pl.BlockSpec((1, HALO, D_MODEL), lambda bb, i: (bb, jnp.minimum((i + 1) * hb, n // HALO - 1), 0)),
            pl.BlockSpec((1, 1, D_MODEL), lambda bb, i: (mod_row(bb), 0, 0)),
            pl.BlockSpec((1, 1, D_MODEL), lambda bb, i: (mod_row(bb), 0, 1)),
            pl.BlockSpec((1, D_MODEL), const),
            pl.BlockSpec((None, D_MODEL, D_IN), lambda bb, i: (layer, 0, 0)),
            pl.BlockSpec((2, tb // GRID_W, LANES), lambda bb, i: (0, i, 0)),
            pl.BlockSpec((2, GRID_W, LANES), lambda bb, i: (0, 0, 0)),
            pl.BlockSpec((1, LANES), const),
            pl.BlockSpec((1, LANES), const),
            pl.BlockSpec((SHORT_W, 8, D_CONV), lambda bb, i: (0, 0, 0)),
            pl.BlockSpec((CONF_W, 8, D_CONV), lambda bb, i: (0, 0, 0)),
            pl.BlockSpec((1, D_CONV), const),
            pl.BlockSpec((1, D_CONV), const),
            pl.BlockSpec((1, D_CONV), const),
        ],
        out_specs=[
            pl.BlockSpec((1, tb, 2 * D_CONV), lambda bb, i: (bb, i, 0)),
            pl.BlockSpec((1, tb, D_ATTN), lambda bb, i: (bb, i, 0)),
            pl.BlockSpec((1, tb, 4 * LANES), lambda bb, i: (bb, i, 0)),
        ],
        out_shape=[
            jax.ShapeDtypeStruct((b, n, 2 * D_CONV), BF16),
            jax.ShapeDtypeStruct((b, n, D_ATTN), BF16),
            jax.ShapeDtypeStruct((b, n, 4 * LANES), BF16),
        ],
        scratch_shapes=[pltpu.VMEM((tb + 2 * HALO, 3 * D_CONV), F32),
                        pltpu.VMEM((tb, LANES), F32), pltpu.VMEM((tb, LANES), F32)],
        compiler_params=pltpu.CompilerParams(
            dimension_semantics=("parallel", "parallel"), vmem_limit_bytes=VMEM_LIMIT),
        name="proj",
    )(x, x, x, mods, mods, norm_g, w_in_b, rot_row, rot_col, gq, gk, conv_a_w, conv_c_w, conv_c_b, ln_g, ln_b)


def _mix_kernel(local, tb, seq,
                x_ref, y_ref, q_ref, kv_ref, kvp_ref, kvn_ref, ckv_ref, sink_ref, wo_ref, g1_ref,
                o_ref, kvbuf, ybuf):
    i = pl.program_id(1)
    lane = lax.broadcasted_iota(jnp.int32, (1, LANES), 1)
    low_head = lane < HEAD_DIM
    if local:
        kvbuf[0:WINDOW, :] = kvp_ref[0]
        kvbuf[WINDOW:WINDOW + tb, :] = kv_ref[0]
        kvbuf[WINDOW + tb:WINDOW + tb + WINDOW, :] = kvn_ref[0]
        n_loc = 3 * WINDOW
        rel = (lax.broadcasted_iota(jnp.int32, (WINDOW, WINDOW), 1)
               - lax.broadcasted_iota(jnp.int32, (WINDOW, WINDOW), 0))
    ones_cols = jnp.ones((1, LANES), BF16)

    def with_ones(v, keep_low):
        zero = jnp.zeros_like(v)
        masked = jnp.where(low_head, v, zero) if keep_low else jnp.where(low_head, zero, v)
        return jnp.concatenate([masked, jnp.broadcast_to(ones_cols, v.shape)], axis=1)

    def project(j):
        j0 = pl.multiple_of(j * WINDOW, WINDOW)
        out = _dot(ybuf[pl.ds(j0, WINDOW), :], wo_ref[...])
        o_ref[0, pl.ds(j0, WINDOW), :] = x_ref[0, pl.ds(j0, WINDOW), :] + g1_ref[0] * out

    def mixer_rows(j, prev=None):
        j0 = pl.multiple_of(j * WINDOW, WINDOW)
        yj = y_ref[0, pl.ds(j0, WINDOW), :]
        y_parts = [yj[:, 0:D_CONV]]

        qj = q_ref[0, pl.ds(j0, WINDOW), :]
        if local:
            base = i * tb + j0
            big = jnp.int32(1 << 20)
            prev_ok = rel >= jnp.where(base >= WINDOW, 0, big)
            next_ok = rel <= jnp.where(base + 2 * WINDOW <= seq, 0, -big)
        scores = []
        for kvh in range(N_KV):
            parts = []
            for g in range(GROUP):
                h = kvh * GROUP + g
                slab = qj[:, (h // 2) * LANES:(h // 2 + 1) * LANES]
                parts.append(jnp.where(low_head if h % 2 == 0 else jnp.logical_not(low_head),
                                       slab, jnp.zeros_like(slab)))
            qs = jnp.concatenate(parts, axis=0)
            kc = ckv_ref[0, :, kvh * LANES:(kvh + 1) * LANES]
            vc = ckv_ref[0, :, (N_KV + kvh) * LANES:(N_KV + kvh + 1) * LANES]
            s_ctx = _dot_nt(qs, kc)
            vcs = (with_ones(vc, True), with_ones(vc, False))
            s_loc = vws = None
            if local:
                kw = kvbuf[pl.ds(j0, n_loc), kvh * LANES:(kvh + 1) * LANES]
                vw = kvbuf[pl.ds(j0, n_loc), (N_KV + kvh) * LANES:(N_KV + kvh + 1) * LANES]
                s_loc = _dot_nt(qs, kw)
                vws = (with_ones(vw, True), with_ones(vw, False))
            scores.append((s_ctx, vcs, s_loc, vws))
            if kvh == 0 and prev is not None:
                project(prev)
        for kvh in range(N_KV):
            s_ctx, vcs, s_loc, vws = scores[kvh]
            for pair in range(GROUP // 2):
                num, den = [], []
                for half in range(2):
                    g = 2 * pair + half
                    sk = sink_ref[kvh * GROUP + g] * LOG2E
                    sc = s_ctx[g * WINDOW:(g + 1) * WINDOW]
                    if local:
                        sg = s_loc[g * WINDOW:(g + 1) * WINDOW]
                        sl = jnp.concatenate([
                            jnp.where(prev_ok, sg[:, 0:WINDOW], NEG_INF),
                            sg[:, WINDOW:2 * WINDOW],
                            jnp.where(next_ok, sg[:, 2 * WINDOW:3 * WINDOW], NEG_INF)], axis=1)
                        m = jnp.maximum(jnp.maximum(jnp.max(sl, axis=-1, keepdims=True),
                                                    jnp.max(sc, axis=-1, keepdims=True)), sk)
                        o = (_dot(jnp.exp2(sl - m).astype(BF16), vws[half])
                             + _dot(jnp.exp2(sc - m).astype(BF16), vcs[half]))
                    else:
                        m = jnp.maximum(jnp.max(sc, axis=-1, keepdims=True), sk)
                        o = _dot(jnp.exp2(sc - m).astype(BF16), vcs[half])
                    num.append(o[:, 0:LANES])
                    den.append(o[:, LANES:2 * LANES] + jnp.exp2(sk - m))
                o_pair = (num[0] + num[1]) / jnp.where(low_head, den[0], den[1])
                y_parts.append(o_pair.astype(BF16))
        y_parts.append(yj[:, D_CONV:2 * D_CONV])
        return jnp.concatenate(y_parts, axis=1)

    n_blocks = tb // WINDOW
    ybuf[0:WINDOW, :] = mixer_rows(0)

    for j in range(1, n_blocks):
        ybuf[j * WINDOW:(j + 1) * WINDOW, :] = mixer_rows(j, j - 1)
    project(n_blocks - 1)


def _mix(x, y, q, kv, ckv, mods, mod_row, sink, w_out_b, layer, tb, local):
    b, n, _ = x.shape
    n_ctx = ckv.shape[1]
    const = lambda bb, i: (0, 0)
    wb = tb // WINDOW
    kern = functools.partial(_mix_kernel, local, tb, n)
    return pl.pallas_call(
        kern,
        grid=(b, n // tb),
        in_specs=[
            pl.BlockSpec((1, tb, D_MODEL), lambda bb, i: (bb, i, 0)),
            pl.BlockSpec((1, tb, 2 * D_CONV), lambda bb, i: (bb, i, 0)),
            pl.BlockSpec((1, tb, D_ATTN), lambda bb, i: (bb, i, 0)),
            pl.BlockSpec((1, tb, 4 * LANES), lambda bb, i: (bb, i, 0)),
            pl.BlockSpec((1, WINDOW, 4 * LANES), lambda bb, i: (bb, jnp.maximum(i * wb - 1, 0), 0)),
            pl.BlockSpec((1, WINDOW, 4 * LANES),
                         lambda bb, i: (bb, jnp.minimum((i + 1) * wb, n // WINDOW - 1), 0)),
            pl.BlockSpec((1, n_ctx, 4 * LANES), lambda bb, i: (bb, 0, 0)),
            pl.BlockSpec(memory_space=pltpu.SMEM),
            pl.BlockSpec((None, D_MODEL, D_MODEL), lambda bb, i: (layer, 0, 0)),
            pl.BlockSpec((1, 1, D_MODEL), lambda bb, i: (mod_row(bb), 0, 2)),
        ],
        out_specs=pl.BlockSpec((1, tb, D_MODEL), lambda bb, i: (bb, i, 0)),
        out_shape=jax.ShapeDtypeStruct((b, n, D_MODEL), F32),
        scratch_shapes=[
            pltpu.VMEM((tb + 2 * WINDOW, 4 * LANES), BF16),
            pltpu.VMEM((tb, D_MODEL), BF16),
        ],
        compiler_params=pltpu.CompilerParams(
            dimension_semantics=("parallel", "parallel"), vmem_limit_bytes=VMEM_LIMIT),
        name="mix_local" if local else "mix_ctx",
    )(x, y, q, kv, kv, kv, ckv, sink, w_out_b, mods)


def _mlp_kernel(x_ref, sh_ref, sc_ref, gate_ref, g_ref, w1_ref, w2_ref, o_ref):
    x = x_ref[0]
    ms = jnp.mean(x * x, axis=-1, keepdims=True)
    h = (x * lax.rsqrt(ms + EPS)) * g_ref[...]
    h = h * (1.0 + sc_ref[0]) + sh_ref[0]
    hb = h.astype(BF16)
    acc = None
    for c in range(D_FF // FF_CHUNK):
        hid = _dot(hb, w1_ref[:, c * FF_CHUNK:(c + 1) * FF_CHUNK])
        r = jnp.maximum(hid, 0.0)
        part = _dot((r * r).astype(BF16), w2_ref[c * FF_CHUNK:(c + 1) * FF_CHUNK, :])
        acc = part if acc is None else acc + part
    o_ref[0] = x + gate_ref[0] * acc


def _mlp(x, mods, mod_row, norm_g, w1_b, w2_b, layer, tb):
    b, n, _ = x.shape
    const = lambda bb, i: (0, 0)
    return pl.pallas_call(
        _mlp_kernel,
        grid=(b, n // tb),
        in_specs=[
            pl.BlockSpec((1, tb, D_MODEL), lambda bb, i: (bb, i, 0)),
            pl.BlockSpec((1, 1, D_MODEL), lambda bb, i: (mod_row(bb), 0, 3)),
            pl.BlockSpec((1, 1, D_MODEL), lambda bb, i: (mod_row(bb), 0, 4)),
            pl.BlockSpec((1, 1, D_MODEL), lambda bb, i: (mod_row(bb), 0, 5)),
            pl.BlockSpec((1, D_MODEL), const),
            pl.BlockSpec((None, D_MODEL, D_FF), lambda bb, i: (layer, 0, 0)),
            pl.BlockSpec((None, D_FF, D_MODEL), lambda bb, i: (layer, 0, 0)),
        ],
        out_specs=pl.BlockSpec((1, tb, D_MODEL), lambda bb, i: (bb, i, 0)),
        out_shape=jax.ShapeDtypeStruct((b, n, D_MODEL), F32),
        compiler_params=pltpu.CompilerParams(
            dimension_semantics=("parallel", "parallel"), vmem_limit_bytes=VMEM_LIMIT),
        name="mlp",
    )(x, mods, mods, mods, norm_g, w1_b, w2_b)


def _rope_tables(seq):
    d_axis = HEAD_DIM // 2
    inv_freq = ROPE_BASE ** (-jnp.arange(0, d_axis, 2, dtype=F32) / d_axis)
    reps = LANES // HEAD_DIM

    def table(n_pos, first):
        ang = jnp.arange(n_pos, dtype=jnp.int32).astype(F32)[:, None] * inv_freq[None, :]
        zero = jnp.zeros_like(ang)
        cos = [jnp.cos(ang), jnp.cos(ang), zero, zero]
        sin = [-jnp.sin(ang), jnp.sin(ang), zero, zero]
        if not first:
            cos, sin = cos[2:] + cos[:2], sin[2:] + sin[:2]
        return jnp.stack([jnp.tile(jnp.concatenate(cos, axis=-1), (1, reps)),
                          jnp.tile(jnp.concatenate(sin, axis=-1), (1, reps))])

    return table(seq // GRID_W, True), table(GRID_W, False)


def kernel(x, c, ctx, c_ctx, w_mod, b_mod, norm1_g, w_in, conv_a_w, q_norm_g, k_norm_g, attn_sink,
           conv_c_w, conv_c_b, ln_c_g, ln_c_b, w_out, norm2_g, w_mlp1, w_mlp2):
    batch, seq, _ = x.shape
    n_ctx = ctx.shape[1]
    depth = w_mod.shape[0]
    tb = min(TOKEN_BLOCK, seq)
    tbw = min(WIDE_BLOCK, seq)
    tb_ctx = n_ctx
    ctx_row = batch

    cvec = jnp.concatenate([c, c_ctx[None, :], jnp.zeros((8 - batch - 1, D_MODEL), F32)], axis=0)
    mods_all = _modulation(cvec.T, batch + 1, w_mod, b_mod).reshape(depth, 8, 1, 6 * D_MODEL)

    rot_row, rot_col = _rope_tables(seq)
    rot_row_c = jnp.stack([jnp.ones((n_ctx // GRID_W, LANES), F32), jnp.zeros((n_ctx // GRID_W, LANES), F32)])
    rot_col_c = jnp.stack([jnp.ones((GRID_W, LANES), F32), jnp.zeros((GRID_W, LANES), F32)])
    reps = LANES // HEAD_DIM
    lat_row = lambda bb: bb
    cx_row = lambda bb: ctx_row

    w_in_b = w_in.astype(BF16)
    w_out_b = w_out.astype(BF16)
    w1_b = w_mlp1.astype(BF16)
    w2_b = w_mlp2.astype(BF16)

    xc = ctx
    for l in range(depth):
        last = l == depth - 1
        mods = mods_all[l]
        g1n = norm1_g[l][None, :]
        g2n = norm2_g[l][None, :]
        gq = jnp.tile(q_norm_g[l], reps)[None, :]
        gk = jnp.tile(k_norm_g[l], reps)[None, :]
        conv_args = (jnp.broadcast_to(conv_a_w[l][:, None, :], (SHORT_W, 8, D_CONV)),
                     jnp.broadcast_to(conv_c_w[l][:, None, :], (CONF_W, 8, D_CONV)),
                     conv_c_b[l][None, :], ln_c_g[l][None, :], ln_c_b[l][None, :])

        y_c, q_c, kv_c = _proj(xc, mods, cx_row, g1n, w_in_b, l, rot_row_c, rot_col_c, gq, gk, *conv_args, tb_ctx)
        y, q, kv = _proj(x, mods, lat_row, g1n, w_in_b, l, rot_row, rot_col, gq, gk, *conv_args, tbw)
        x = _mix(x, y, q, kv, kv_c, mods, lat_row, attn_sink[l], w_out_b, l, tbw, True)
        x = _mlp(x, mods, lat_row, g2n, w1_b, w2_b, l, tb)
        if not last:
            xc = _mix(xc, y_c, q_c, kv_c, kv_c, mods, cx_row, attn_sink[l], w_out_b, l, tb_ctx, False)
            xc = _mlp(xc, mods, cx_row, g2n, w1_b, w2_b, l, tb_ctx)
    return x
```

```python
import functools

import jax
import jax.numpy as jnp
from jax import lax
from jax.experimental import pallas as pl
from jax.experimental.pallas import tpu as pltpu

F32 = jnp.float32
BF16 = jnp.bfloat16

D_MODEL = 1024
HEAD_DIM = 64
N_HEADS = 8
N_KV = 2
GROUP = N_HEADS // N_KV
D_CONV = 256
D_ATTN = N_HEADS * HEAD_DIM
D_FF = 4 * D_MODEL
GRID_W = 64
WINDOW = 128
SHORT_W = 3
CONF_W = 31
HALO = 16
ROPE_BASE = 10000.0
EPS = 1e-6
NEG_INF = -1e30
LOG2E = 1.4426950408889634

OFF_Q = 3 * D_CONV
OFF_K = OFF_Q + D_ATTN
OFF_C = OFF_K + 2 * N_KV * HEAD_DIM
D_IN = OFF_C + 2 * D_CONV

LANES = 128
TOKEN_BLOCK = 512
WIDE_BLOCK = 1024
CONV_ROWS = 64
FF_CHUNK = 1024
MOD_COLS = 1536
VMEM_LIMIT = 56 * 1024 * 1024


def _sigmoid(t):
    return 1.0 / (1.0 + jnp.exp(-t))


def _dot(a, b):
    return jnp.dot(a, b, preferred_element_type=F32)


def _dot_nt(a, b):
    return lax.dot_general(a, b, (((1,), (1,)), ((), ())), preferred_element_type=F32)


def _mod_kernel(n_vec, c_ref, w_ref, b_ref, o_ref):
    c = c_ref[...]
    s = c * _sigmoid(c)
    w = w_ref[0]
    rows = [jnp.sum(w * s[:, r:r + 1], axis=0, keepdims=True) for r in range(n_vec)]
    rows.append(jnp.zeros((8 - n_vec, w.shape[1]), F32))
    o_ref[0] = jnp.concatenate(rows, axis=0) + b_ref[0]


def _modulation(cvec_t, n_vec, w_mod, b_mod):
    depth, _, n = w_mod.shape
    return pl.pallas_call(
        functools.partial(_mod_kernel, n_vec),
        grid=(depth, n // MOD_COLS),
        in_specs=[
            pl.BlockSpec((D_MODEL, 8), lambda l, j: (0, 0)),
            pl.BlockSpec((1, D_MODEL, MOD_COLS), lambda l, j: (l, 0, j)),
            pl.BlockSpec((1, 1, MOD_COLS), lambda l, j: (l, 0, j)),
        ],
        out_specs=pl.BlockSpec((1, 8, MOD_COLS), lambda l, j: (l, 0, j)),
        out_shape=jax.ShapeDtypeStruct((depth, 8, n), F32),
        compiler_params=pltpu.CompilerParams(
            dimension_semantics=("arbitrary", "arbitrary"), vmem_limit_bytes=VMEM_LIMIT),
        name="modulation",
    )(cvec_t, w_mod, b_mod.reshape(depth, 1, n))


def _phase_conv(buf, col0, w_ref, n_taps, r0, rows):
    phases = {}
    for k in range(n_taps):
        off = HALO + k - n_taps // 2
        phases.setdefault(off % 8, []).append((off - off % 8, k))
    out = None
    for p in sorted(phases):
        n_rows = rows if p == 0 else rows + 8
        z = None
        for a8, k in phases[p]:
            xs = buf[r0 + a8:r0 + a8 + n_rows, col0:col0 + D_CONV]
            term = (xs.reshape(n_rows // 8, 8, D_CONV) * w_ref[k]).reshape(n_rows, D_CONV)
            z = term if z is None else z + term
        z = z[p:p + rows]
        out = z if out is None else out + z
    return out


def _proj_kernel(tb, seq,
                 x_ref, xp_ref, xn_ref, sh_ref, sc_ref, g_ref, w_ref, rot_row_ref, rot_col_ref, gq_ref, gk_ref,
                 wa_ref, wc_ref, cb_ref, lg_ref, lb_ref,
                 y_ref, q_ref, kv_ref, cbuf, cos_ref, sin_ref):
    i = pl.program_id(1)
    has_prev = i > 0
    has_next = i < seq // tb - 1

    by_row = (lax.broadcasted_iota(jnp.int32, (1, LANES), 1) % HEAD_DIM) < HEAD_DIM // 2
    for r in range(tb // GRID_W):
        cos_ref[r * GRID_W:(r + 1) * GRID_W, :] = jnp.where(by_row, rot_row_ref[0, r:r + 1, :], rot_col_ref[0])
        sin_ref[r * GRID_W:(r + 1) * GRID_W, :] = jnp.where(by_row, rot_row_ref[1, r:r + 1, :], rot_col_ref[1])

    gain = g_ref[...] * (1.0 + sc_ref[0])

    def norm_mod(x):
        ms = jnp.mean(x * x, axis=-1, keepdims=True)
        return ((x * lax.rsqrt(ms + EPS)) * gain + sh_ref[0]).astype(BF16)

    lane = lax.broadcasted_iota(jnp.int32, (1, LANES), 1)
    rot_from_above = (lane % (HEAD_DIM // 2)) < (HEAD_DIM // 4)
    low_head = lane < HEAD_DIM
    r_i = lax.broadcasted_iota(jnp.int32, (LANES, LANES), 0) // HEAD_DIM
    c_i = lax.broadcasted_iota(jnp.int32, (LANES, LANES), 1) // HEAD_DIM
    head_ones = jnp.where(r_i == c_i, 1.0, 0.0).astype(BF16)

    def norm_rope(t, gain, scale, cos, sin):
        ssq = _dot((t * t).astype(BF16), head_ones)
        rs = lax.rsqrt(ssq * (1.0 / HEAD_DIM) + EPS) * scale
        tg = t * gain
        partner = jnp.where(rot_from_above, pltpu.roll(tg, LANES - HEAD_DIM // 4, 1),
                            pltpu.roll(tg, HEAD_DIM // 4, 1))
        return rs * (tg * cos + partner * sin)

    def conv_chunk(c):
        r0 = c * CONV_ROWS
        a = _phase_conv(cbuf, 0, wa_ref, SHORT_W, r0, CONV_ROWS)
        b_gate = cbuf[HALO + r0:HALO + r0 + CONV_ROWS, 2 * D_CONV:3 * D_CONV]
        y_ref[0, r0:r0 + CONV_ROWS, 0:D_CONV] = (b_gate * a).astype(BF16)
        acc = _phase_conv(cbuf, D_CONV, wc_ref, CONF_W, r0, CONV_ROWS) + cb_ref[...]
        mu = jnp.mean(acc, axis=-1, keepdims=True)
        xc = acc - mu
        var = jnp.mean(xc * xc, axis=-1, keepdims=True)
        yn = (xc * lax.rsqrt(var + EPS)) * lg_ref[...] + lb_ref[...]
        y_ref[0, r0:r0 + CONV_ROWS, D_CONV:2 * D_CONV] = (yn * _sigmoid(yn)).astype(BF16)

    half = tb // 2
    pieces = ((0, half + 2 * HALO), (half + 2 * HALO, tb + 2 * HALO))
    main = tuple((max(lo - HALO, 0), min(hi - HALO, tb)) for lo, hi in pieces)

    def normed(p):
        lo, hi = pieces[p]
        parts = [norm_mod(x_ref[0, main[p][0]:main[p][1], :])]
        if lo == 0:
            parts.insert(0, norm_mod(xp_ref[0]))
        if hi == tb + 2 * HALO:
            parts.append(norm_mod(xn_ref[0]))
        return jnp.concatenate(parts, axis=0)

    def main_rows(p, hb):
        return hb[main[p][0] + HALO - pieces[p][0]:main[p][1] + HALO - pieces[p][0]]

    def stage_short(p, hb):
        lo, hi = pieces[p]
        ua = _dot(hb, w_ref[:, 0:OFF_Q])
        cbuf[lo:hi, 0:D_CONV] = ua[:, 2 * D_CONV:3 * D_CONV] * ua[:, 0:D_CONV]
        cbuf[lo:hi, 2 * D_CONV:3 * D_CONV] = ua[:, D_CONV:2 * D_CONV]

    def stage_glu(p, hb):
        lo, hi = pieces[p]
        uc = _dot(hb, w_ref[:, OFF_C:D_IN])
        cbuf[lo:hi, D_CONV:2 * D_CONV] = uc[:, 0:D_CONV] * _sigmoid(uc[:, D_CONV:2 * D_CONV])
        if lo == 0:
            cbuf[0:HALO, 0:2 * D_CONV] = jnp.where(has_prev, cbuf[0:HALO, 0:2 * D_CONV], 0.0)
        if hi == tb + 2 * HALO:
            cbuf[hi - HALO:hi, 0:2 * D_CONV] = jnp.where(has_next, cbuf[hi - HALO:hi, 0:2 * D_CONV], 0.0)

    def q_outputs(p, hb, slabs):
        m_lo, m_hi = main[p]
        s0, s1 = slabs
        uq = _dot(main_rows(p, hb), w_ref[:, OFF_Q + s0 * LANES:OFF_Q + s1 * LANES])
        for s in range(s0, s1):
            qs = norm_rope(uq[:, (s - s0) * LANES:(s - s0 + 1) * LANES], gq_ref[...], LOG2E * HEAD_DIM ** -0.5,
                           cos_ref[m_lo:m_hi, :], sin_ref[m_lo:m_hi, :])
            q_ref[0, m_lo:m_hi, s * LANES:(s + 1) * LANES] = qs.astype(BF16)

    def kv_outputs(p, hb):
        m_lo, m_hi = main[p]
        ukv = _dot(main_rows(p, hb), w_ref[:, OFF_K:OFF_C])
        kn = norm_rope(ukv[:, 0:LANES], gk_ref[...], 1.0, cos_ref[m_lo:m_hi, :], sin_ref[m_lo:m_hi, :])
        for j, t in enumerate((kn, ukv[:, LANES:2 * LANES])):
            swapped = pltpu.roll(t, HEAD_DIM, 1)
            kv_ref[0, m_lo:m_hi, (2 * j) * LANES:(2 * j + 1) * LANES] = jnp.where(low_head, t, swapped).astype(BF16)
            kv_ref[0, m_lo:m_hi, (2 * j + 1) * LANES:(2 * j + 2) * LANES] = (
                jnp.where(low_head, swapped, t).astype(BF16))

    for p, (lo, hi) in enumerate(pieces):
        hb = normed(p)
        stage_short(p, hb)
        stage_glu(p, hb)
        q_outputs(p, hb, (0, D_ATTN // LANES))
        kv_outputs(p, hb)
        for c in range(tb // CONV_ROWS):
            if lo < (c + 1) * CONV_ROWS + 2 * HALO <= hi:
                conv_chunk(c)


def _proj(x, mods, mod_row, norm_g, w_in_b, layer, rot_row, rot_col, gq, gk, conv_a_w, conv_c_w, conv_c_b, ln_g,
          ln_b, tb):
    b, n, _ = x.shape
    const = lambda bb, i: (0, 0)
    hb = tb // HALO
    kern = functools.partial(_proj_kernel, tb, n)
    return pl.pallas_call(
        kern,
        grid=(b, n // tb),
        in_specs=[
            pl.BlockSpec((1, tb, D_MODEL), lambda bb, i: (bb, i, 0)),
            pl.BlockSpec((1, HALO, D_MODEL), lambda bb, i: (bb, jnp.maximum(i * hb - 1, 0), 0)),
            pl.BlockSpec((1, HALO, D_MODEL), lambda bb, i: (bb, jnp.minimum((i + 1) * hb, n // HALO - 1), 0)),
            pl.BlockSpec((1, 1, D_MODEL), lambda bb, i: (mod_row(bb), 0, 0)),
            pl.BlockSpec((1, 1, D_MODEL), lambda bb, i: (mod_row(bb), 0, 1)),
            pl.BlockSpec((1, D_MODEL), const),
            pl.BlockSpec((None, D_MODEL, D_IN), lambda bb, i: (layer, 0, 0)),
            pl.BlockSpec((2, tb // GRID_W, LANES), lambda bb, i: (0, i, 0)),
            pl.BlockSpec((2, GRID_W, LANES), lambda bb, i: (0, 0, 0)),
            pl.BlockSpec((1, LANES), const),
            pl.BlockSpec((1, LANES), const),
            pl.BlockSpec((SHORT_W, 8, D_CONV), lambda bb, i: (0, 0, 0)),
            pl.BlockSpec((CONF_W, 8, D_CONV), lambda bb, i: (0, 0, 0)),
            pl.BlockSpec((1, D_CONV), const),
            pl.BlockSpec((1, D_CONV), const),
            pl.BlockSpec((1, D_CONV), const),
        ],
        out_specs=[
            pl.BlockSpec((1, tb, 2 * D_CONV), lambda bb, i: (bb, i, 0)),
            pl.BlockSpec((1, tb, D_ATTN), lambda bb, i: (bb, i, 0)),
            pl.BlockSpec((1, tb, 4 * LANES), lambda bb, i: (bb, i, 0)),
        ],
        out_shape=[
            jax.ShapeDtypeStruct((b, n, 2 * D_CONV), BF16),
            jax.ShapeDtypeStruct((b, n, D_ATTN), BF16),
            jax.ShapeDtypeStruct((b, n, 4 * LANES), BF16),
        ],
        scratch_shapes=[pltpu.VMEM((tb + 2 * HALO, 3 * D_CONV), F32),
                        pltpu.VMEM((tb, LANES), F32), pltpu.VMEM((tb, LANES), F32)],
        compiler_params=pltpu.CompilerParams(
            dimension_semantics=("parallel", "parallel"), vmem_limit_bytes=VMEM_LIMIT),
        name="proj",
    )(x, x, x, mods, mods, norm_g, w_in_b, rot_row, rot_col, gq, gk, conv_a_w, conv_c_w, conv_c_b, ln_g, ln_b)


def _mix_kernel(local, tb, seq,
                x_ref, y_ref, q_ref, kv_ref, kvp_ref, kvn_ref, ckv_ref, sink_ref, wo_ref, g1_ref,
                o_ref, kvbuf, ybuf):
    i = pl.program_id(1)
    lane = lax.broadcasted_iota(jnp.int32, (1, LANES), 1)
    low_head = lane < HEAD_DIM
    if local:
        kvbuf[0:WINDOW, :] = kvp_ref[0]
        kvbuf[WINDOW:WINDOW + tb, :] = kv_ref[0]
        kvbuf[WINDOW + tb:WINDOW + tb + WINDOW, :] = kvn_ref[0]
        n_loc = 3 * WINDOW
        rel = (lax.broadcasted_iota(jnp.int32, (WINDOW, WINDOW), 1)
               - lax.broadcasted_iota(jnp.int32, (WINDOW, WINDOW), 0))
    ones_cols = jnp.ones((1, LANES), BF16)

    def with_ones(v, keep_low):
        zero = jnp.zeros_like(v)
        masked = jnp.where(low_head, v, zero) if keep_low else jnp.where(low_head, zero, v)
        return jnp.concatenate([masked, jnp.broadcast_to(ones_cols, v.shape)], axis=1)

    def project(j):
        j0 = pl.multiple_of(j * WINDOW, WINDOW)
        out = _dot(ybuf[pl.ds(j0, WINDOW), :], wo_ref[...])
        o_ref[0, pl.ds(j0, WINDOW), :] = x_ref[0, pl.ds(j0, WINDOW), :] + g1_ref[0] * out

    def block_scores(j, kvh):
        j0 = j * WINDOW
        qj = q_ref[0, j0:j0 + WINDOW, :]
        parts = []
        for g in range(GROUP):
            h = kvh * GROUP + g
            slab = qj[:, (h // 2) * LANES:(h // 2 + 1) * LANES]
            parts.append(jnp.where(low_head if h % 2 == 0 else jnp.logical_not(low_head),
                                   slab, jnp.zeros_like(slab)))
        qs = jnp.concatenate(parts, axis=0)
        kc = ckv_ref[0, :, kvh * LANES:(kvh + 1) * LANES]
        vc = ckv_ref[0, :, (N_KV + kvh) * LANES:(N_KV + kvh + 1) * LANES]
        s_ctx = _dot_nt(qs, kc)
        vcs = (with_ones(vc, True), with_ones(vc, False))
        s_loc = vws = None
        if local:
            kw = kvbuf[j0:j0 + n_loc, kvh * LANES:(kvh + 1) * LANES]
            vw = kvbuf[j0:j0 + n_loc, (N_KV + kvh) * LANES:(N_KV + kvh + 1) * LANES]
            s_loc = _dot_nt(qs, kw)
            vws = (with_ones(vw, True), with_ones(vw, False))
        return s_ctx, vcs, s_loc, vws

    def mixer_rows(j, scores, fillers):
        j0 = j * WINDOW
        yj = y_ref[0, j0:j0 + WINDOW, :]
        y_parts = [yj[:, 0:D_CONV]]
        if local:
            base = i * tb + j0
            big = jnp.int32(1 << 20)
            prev_ok = rel >= jnp.where(base >= WINDOW, 0, big)
            next_ok = rel <= jnp.where(base + 2 * WINDOW <= seq, 0, -big)
        if 0 in fillers:
            fillers[0]()
        n_pv = 0
        for kvh in range(N_KV):
            s_ctx, vcs, s_loc, vws = scores[kvh]
            for pair in range(GROUP // 2):
                num, den = [], []
                for half in range(2):
                    g = 2 * pair + half
                    sk = sink_ref[kvh * GROUP + g] * LOG2E
                    sc = s_ctx[g * WINDOW:(g + 1) * WINDOW]
                    if local:
                        sg = s_loc[g * WINDOW:(g + 1) * WINDOW]
                        sl = jnp.concatenate([
                            jnp.where(prev_ok, sg[:, 0:WINDOW], NEG_INF),
                            sg[:, WINDOW:2 * WINDOW],
                            jnp.where(next_ok, sg[:, 2 * WINDOW:3 * WINDOW], NEG_INF)], axis=1)
                        m = jnp.maximum(jnp.maximum(jnp.max(sl, axis=-1, keepdims=True),
                                                    jnp.max(sc, axis=-1, keepdims=True)), sk)
                        o = (_dot(jnp.exp2(sl - m).astype(BF16), vws[half])
                             + _dot(jnp.exp2(sc - m).astype(BF16), vcs[half]))
                    else:
                        m = jnp.maximum(jnp.max(sc, axis=-1, keepdims=True), sk)
                        o = _dot(jnp.exp2(sc - m).astype(BF16), vcs[half])
                    n_pv += 1
                    if n_pv in fillers:
                        fillers[n_pv]()
                    num.append(o[:, 0:LANES])
                    den.append(o[:, LANES:2 * LANES] + jnp.exp2(sk - m))
                o_pair = (num[0] + num[1]) / jnp.where(low_head, den[0], den[1])
                y_parts.append(o_pair.astype(BF16))
        y_parts.append(yj[:, D_CONV:2 * D_CONV])
        return jnp.concatenate(y_parts, axis=1)

    n_blocks = tb // WINDOW
    scores = [block_scores(0, kvh) for kvh in range(N_KV)]
    for j in range(n_blocks):
        nxt = []
        fillers = {}
        if j > 0:
            fillers[0] = functools.partial(project, j - 1)
        if j + 1 < n_blocks:
            fillers[GROUP] = lambda j=j, nxt=nxt: nxt.append(block_scores(j + 1, 0))
            fillers[2 * GROUP] = lambda j=j, nxt=nxt: nxt.append(block_scores(j + 1, 1))
        ybuf[j * WINDOW:(j + 1) * WINDOW, :] = mixer_rows(j, scores, fillers)
        scores = nxt
    project(n_blocks - 1)


def _mix(x, y, q, kv, ckv, mods, mod_row, sink, w_out_b, layer, tb, local):
    b, n, _ = x.shape
    n_ctx = ckv.shape[1]
    const = lambda bb, i: (0, 0)
    wb = tb // WINDOW
    kern = functools.partial(_mix_kernel, local, tb, n)
    return pl.pallas_call(
        kern,
        grid=(b, n // tb),
        in_specs=[
            pl.BlockSpec((1, tb, D_MODEL), lambda bb, i: (bb, i, 0)),
            pl.BlockSpec((1, tb, 2 * D_CONV), lambda bb, i: (bb, i, 0)),
            pl.BlockSpec((1, tb, D_ATTN), lambda bb, i: (bb, i, 0)),
            pl.BlockSpec((1, tb, 4 * LANES), lambda bb, i: (bb, i, 0)),
            pl.BlockSpec((1, WINDOW, 4 * LANES), lambda bb, i: (bb, jnp.maximum(i * wb - 1, 0), 0)),
            pl.BlockSpec((1, WINDOW, 4 * LANES),
                         lambda bb, i: (bb, jnp.minimum((i + 1) * wb, n // WINDOW - 1), 0)),
            pl.BlockSpec((1, n_ctx, 4 * LANES), lambda bb, i: (bb, 0, 0)),
            pl.BlockSpec(memory_space=pltpu.SMEM),
            pl.BlockSpec((None, D_MODEL, D_MODEL), lambda bb, i: (layer, 0, 0)),
            pl.BlockSpec((1, 1, D_MODEL), lambda bb, i: (mod_row(bb), 0, 2)),
        ],
        out_specs=pl.BlockSpec((1, tb, D_MODEL), lambda bb, i: (bb, i, 0)),
        out_shape=jax.ShapeDtypeStruct((b, n, D_MODEL), F32),
        scratch_shapes=[
            pltpu.VMEM((tb + 2 * WINDOW, 4 * LANES), BF16),
            pltpu.VMEM((tb, D_MODEL), BF16),
        ],
        compiler_params=pltpu.CompilerParams(
            dimension_semantics=("parallel", "parallel"), vmem_limit_bytes=VMEM_LIMIT),
        name="mix_local" if local else "mix_ctx",
    )(x, y, q, kv, kv, kv, ckv, sink, w_out_b, mods)


def _mlp_kernel(x_ref, sh_ref, sc_ref, gate_ref, g_ref, w1_ref, w2_ref, o_ref):
    x = x_ref[0]
    ms = jnp.mean(x * x, axis=-1, keepdims=True)
    h = (x * lax.rsqrt(ms + EPS)) * g_ref[...]
    h = h * (1.0 + sc_ref[0]) + sh_ref[0]
    hb = h.astype(BF16)
    acc = None
    for c in range(D_FF // FF_CHUNK):
        hid = _dot(hb, w1_ref[:, c * FF_CHUNK:(c + 1) * FF_CHUNK])
        r = jnp.maximum(hid, 0.0)
        part = _dot((r * r).astype(BF16), w2_ref[c * FF_CHUNK:(c + 1) * FF_CHUNK, :])
        acc = part if acc is None else acc + part
    o_ref[0] = x + gate_ref[0] * acc


def _mlp(x, mods, mod_row, norm_g, w1_b, w2_b, layer, tb):
    b, n, _ = x.shape
    const = lambda bb, i: (0, 0)
    return pl.pallas_call(
        _mlp_kernel,
        grid=(b, n // tb),
        in_specs=[
            pl.BlockSpec((1, tb, D_MODEL), lambda bb, i: (bb, i, 0)),
            pl.BlockSpec((1, 1, D_MODEL), lambda bb, i: (mod_row(bb), 0, 3)),
            pl.BlockSpec((1, 1, D_MODEL), lambda bb, i: (mod_row(bb), 0, 4)),
            pl.BlockSpec((1, 1, D_MODEL), lambda bb, i: (mod_row(bb), 0, 5)),
            pl.BlockSpec((1, D_MODEL), const),
            pl.BlockSpec((None, D_MODEL, D_FF), lambda bb, i: (layer, 0, 0)),
            pl.BlockSpec((None, D_FF, D_MODEL), lambda bb, i: (layer, 0, 0)),
        ],
        out_specs=pl.BlockSpec((1, tb, D_MODEL), lambda bb, i: (bb, i, 0)),
        out_shape=jax.ShapeDtypeStruct((b, n, D_MODEL), F32),
        compiler_params=pltpu.CompilerParams(
            dimension_semantics=("parallel", "parallel"), vmem_limit_bytes=VMEM_LIMIT),
        name="mlp",
    )(x, mods, mods, mods, norm_g, w1_b, w2_b)


def _rope_tables(seq):
    d_axis = HEAD_DIM // 2
    inv_freq = ROPE_BASE ** (-jnp.arange(0, d_axis, 2, dtype=F32) / d_axis)
    reps = LANES // HEAD_DIM

    def table(n_pos, first):
        ang = jnp.arange(n_pos, dtype=jnp.int32).astype(F32)[:, None] * inv_freq[None, :]
        zero = jnp.zeros_like(ang)
        cos = [jnp.cos(ang), jnp.cos(ang), zero, zero]
        sin = [-jnp.sin(ang), jnp.sin(ang), zero, zero]
        if not first:
            cos, sin = cos[2:] + cos[:2], sin[2:] + sin[:2]
        return jnp.stack([jnp.tile(jnp.concatenate(cos, axis=-1), (1, reps)),
                          jnp.tile(jnp.concatenate(sin, axis=-1), (1, reps))])

    return table(seq // GRID_W, True), table(GRID_W, False)


def kernel(x, c, ctx, c_ctx, w_mod, b_mod, norm1_g, w_in, conv_a_w, q_norm_g, k_norm_g, attn_sink,
           conv_c_w, conv_c_b, ln_c_g, ln_c_b, w_out, norm2_g, w_mlp1, w_mlp2):
    batch, seq, _ = x.shape
    n_ctx = ctx.shape[1]
    depth = w_mod.shape[0]
    tb = min(TOKEN_BLOCK, seq)
    tbw = min(WIDE_BLOCK, seq)
    tb_ctx = n_ctx
    ctx_row = batch

    cvec = jnp.concatenate([c, c_ctx[None, :], jnp.zeros((8 - batch - 1, D_MODEL), F32)], axis=0)
    mods_all = _modulation(cvec.T, batch + 1, w_mod, b_mod).reshape(depth, 8, 1, 6 * D_MODEL)

    rot_row, rot_col = _rope_tables(seq)
    rot_row_c = jnp.stack([jnp.ones((n_ctx // GRID_W, LANES), F32), jnp.zeros((n_ctx // GRID_W, LANES), F32)])
    rot_col_c = jnp.stack([jnp.ones((GRID_W, LANES), F32), jnp.zeros((GRID_W, LANES), F32)])
    reps = LANES // HEAD_DIM
    lat_row = lambda bb: bb
    cx_row = lambda bb: ctx_row

    w_in_b = w_in.astype(BF16)
    w_out_b = w_out.astype(BF16)
    w1_b = w_mlp1.astype(BF16)
    w2_b = w_mlp2.astype(BF16)

    xc = ctx
    for l in range(depth):
        last = l == depth - 1
        mods = mods_all[l]
        g1n = norm1_g[l][None, :]
        g2n = norm2_g[l][None, :]
        gq = jnp.tile(q_norm_g[l], reps)[None, :]
        gk = jnp.tile(k_norm_g[l], reps)[None, :]
        conv_args = (jnp.broadcast_to(conv_a_w[l][:, None, :], (SHORT_W, 8, D_CONV)),
                     jnp.broadcast_to(conv_c_w[l][:, None, :], (CONF_W, 8, D_CONV)),
                     conv_c_b[l][None, :], ln_c_g[l][None, :], ln_c_b[l][None, :])

        y_c, q_c, kv_c = _proj(xc, mods, cx_row, g1n, w_in_b, l, rot_row_c, rot_col_c, gq, gk, *conv_args, tb_ctx)
        y, q, kv = _proj(x, mods, lat_row, g1n, w_in_b, l, rot_row, rot_col, gq, gk, *conv_args, tbw)
        x = _mix(x, y, q, kv, kv_c, mods, lat_row, attn_sink[l], w_out_b, l, tbw, True)
        x = _mlp(x, mods, lat_row, g2n, w1_b, w2_b, l, tb)
        if not last:
            xc = _mix(xc, y_c, q_c, kv_c, kv_c, mods, cx_row, attn_sink[l], w_out_b, l, tb_ctx, False)
            xc = _mlp(xc, mods, cx_row, g2n, w1_b, w2_b, l, tb_ctx)
    return x
```

```python
import functools

import jax
import jax.numpy as jnp
from jax import lax
from jax.experimental import pallas as pl
from jax.experimental.pallas import tpu as pltpu

F32 = jnp.float32
BF16 = jnp.bfloat16

D_MODEL = 1024
HEAD_DIM = 64
N_HEADS = 8
N_KV = 2
GROUP = N_HEADS // N_KV
D_CONV = 256
D_ATTN = N_HEADS * HEAD_DIM
D_FF = 4 * D_MODEL
GRID_W = 64
WINDOW = 128
SHORT_W = 3
CONF_W = 31
HALO = 16
ROPE_BASE = 10000.0
EPS = 1e-6
NEG_INF = -1e30
LOG2E = 1.4426950408889634

OFF_Q = 3 * D_CONV
OFF_K = OFF_Q + D_ATTN
OFF_C = OFF_K + 2 * N_KV * HEAD_DIM
D_IN = OFF_C + 2 * D_CONV

LANES = 128
TOKEN_BLOCK = 512
WIDE_BLOCK = 1024
CONV_ROWS = 64
FF_CHUNK = 1024
MOD_COLS = 1536
VMEM_LIMIT = 56 * 1024 * 1024


def _sigmoid(t):
    return 1.0 / (1.0 + jnp.exp(-t))


def _dot(a, b):
    return jnp.dot(a, b, preferred_element_type=F32)


def _dot_nt(a, b):
    return lax.dot_general(a, b, (((1,), (1,)), ((), ())), preferred_element_type=F32)


def _mod_kernel(n_vec, c_ref, w_ref, b_ref, o_ref):
    c = c_ref[...]
    s = c * _sigmoid(c)
    w = w_ref[0]
    rows = [jnp.sum(w * s[:, r:r + 1], axis=0, keepdims=True) for r in range(n_vec)]
    rows.append(jnp.zeros((8 - n_vec, w.shape[1]), F32))
    o_ref[0] = jnp.concatenate(rows, axis=0) + b_ref[0]


def _modulation(cvec_t, n_vec, w_mod, b_mod):
    depth, _, n = w_mod.shape
    return pl.pallas_call(
        functools.partial(_mod_kernel, n_vec),
        grid=(depth, n // MOD_COLS),
        in_specs=[
            pl.BlockSpec((D_MODEL, 8), lambda l, j: (0, 0)),
            pl.BlockSpec((1, D_MODEL, MOD_COLS), lambda l, j: (l, 0, j)),
            pl.BlockSpec((1, 1, MOD_COLS), lambda l, j: (l, 0, j)),
        ],
        out_specs=pl.BlockSpec((1, 8, MOD_COLS), lambda l, j: (l, 0, j)),
        out_shape=jax.ShapeDtypeStruct((depth, 8, n), F32),
        compiler_params=pltpu.CompilerParams(
            dimension_semantics=("arbitrary", "arbitrary"), vmem_limit_bytes=VMEM_LIMIT),
        name="modulation",
    )(cvec_t, w_mod, b_mod.reshape(depth, 1, n))


def _phase_conv(buf, col0, w_ref, n_taps, r0, rows):
    phases = {}
    for k in range(n_taps):
        off = HALO + k - n_taps // 2
        phases.setdefault(off % 8, []).append((off - off % 8, k))
    out = None
    for p in sorted(phases):
        n_rows = rows if p == 0 else rows + 8
        z = None
        for a8, k in phases[p]:
            xs = buf[r0 + a8:r0 + a8 + n_rows, col0:col0 + D_CONV]
            term = (xs.reshape(n_rows // 8, 8, D_CONV) * w_ref[k]).reshape(n_rows, D_CONV)
            z = term if z is None else z + term
        z = z[p:p + rows]
        out = z if out is None else out + z
    return out


def _proj_kernel(tb, seq,
                 x_ref, xp_ref, xn_ref, sh_ref, sc_ref, g_ref, w_ref, rot_row_ref, rot_col_ref, gq_ref, gk_ref,
                 wa_ref, wc_ref, cb_ref, lg_ref, lb_ref,
                 y_ref, q_ref, kv_ref, cbuf, cos_ref, sin_ref):
    i = pl.program_id(1)
    has_prev = i > 0
    has_next = i < seq // tb - 1

    by_row = (lax.broadcasted_iota(jnp.int32, (1, LANES), 1) % HEAD_DIM) < HEAD_DIM // 2
    for r in range(tb // GRID_W):
        cos_ref[r * GRID_W:(r + 1) * GRID_W, :] = jnp.where(by_row, rot_row_ref[0, r:r + 1, :], rot_col_ref[0])
        sin_ref[r * GRID_W:(r + 1) * GRID_W, :] = jnp.where(by_row, rot_row_ref[1, r:r + 1, :], rot_col_ref[1])

    gain = g_ref[...] * (1.0 + sc_ref[0])

    def norm_mod(x):
        ms = jnp.mean(x * x, axis=-1, keepdims=True)
        return ((x * lax.rsqrt(ms + EPS)) * gain + sh_ref[0]).astype(BF16)

    lane = lax.broadcasted_iota(jnp.int32, (1, LANES), 1)
    rot_from_above = (lane % (HEAD_DIM // 2)) < (HEAD_DIM // 4)
    low_head = lane < HEAD_DIM
    r_i = lax.broadcasted_iota(jnp.int32, (LANES, LANES), 0) // HEAD_DIM
    c_i = lax.broadcasted_iota(jnp.int32, (LANES, LANES), 1) // HEAD_DIM
    head_ones = jnp.where(r_i == c_i, 1.0, 0.0).astype(BF16)

    def norm_rope(t, gain, scale, cos, sin):
        ssq = _dot((t * t).astype(BF16), head_ones)
        rs = lax.rsqrt(ssq * (1.0 / HEAD_DIM) + EPS) * scale
        tg = t * gain
        partner = jnp.where(rot_from_above, pltpu.roll(tg, LANES - HEAD_DIM // 4, 1),
                            pltpu.roll(tg, HEAD_DIM // 4, 1))
        return rs * (tg * cos + partner * sin)

    def conv_chunk(c):
        r0 = c * CONV_ROWS
        a = _phase_conv(cbuf, 0, wa_ref, SHORT_W, r0, CONV_ROWS)
        b_gate = cbuf[HALO + r0:HALO + r0 + CONV_ROWS, 2 * D_CONV:3 * D_CONV]
        y_ref[0, r0:r0 + CONV_ROWS, 0:D_CONV] = (b_gate * a).astype(BF16)
        acc = _phase_conv(cbuf, D_CONV, wc_ref, CONF_W, r0, CONV_ROWS) + cb_ref[...]
        mu = jnp.mean(acc, axis=-1, keepdims=True)
        xc = acc - mu
        var = jnp.mean(xc * xc, axis=-1, keepdims=True)
        yn = (xc * lax.rsqrt(var + EPS)) * lg_ref[...] + lb_ref[...]
        y_ref[0, r0:r0 + CONV_ROWS, D_CONV:2 * D_CONV] = (yn * _sigmoid(yn)).astype(BF16)

    half = tb // 2
    pieces = ((0, half + 2 * HALO), (half + 2 * HALO, tb + 2 * HALO))
    main = tuple((max(lo - HALO, 0), min(hi - HALO, tb)) for lo, hi in pieces)

    def normed(p):
        lo, hi = pieces[p]
        parts = [norm_mod(x_ref[0, main[p][0]:main[p][1], :])]
        if lo == 0:
            parts.insert(0, norm_mod(xp_ref[0]))
        if hi == tb + 2 * HALO:
            parts.append(norm_mod(xn_ref[0]))
        return jnp.concatenate(parts, axis=0)

    def main_rows(p, hb):
        return hb[main[p][0] + HALO - pieces[p][0]:main[p][1] + HALO - pieces[p][0]]

    def stage_short(p, hb):
        lo, hi = pieces[p]
        ua = _dot(hb, w_ref[:, 0:OFF_Q])
        cbuf[lo:hi, 0:D_CONV] = ua[:, 2 * D_CONV:3 * D_CONV] * ua[:, 0:D_CONV]
        cbuf[lo:hi, 2 * D_CONV:3 * D_CONV] = ua[:, D_CONV:2 * D_CONV]

    def stage_glu(p, hb):
        lo, hi = pieces[p]
        uc = _dot(hb, w_ref[:, OFF_C:D_IN])
        cbuf[lo:hi, D_CONV:2 * D_CONV] = uc[:, 0:D_CONV] * _sigmoid(uc[:, D_CONV:2 * D_CONV])
        if lo == 0:
            cbuf[0:HALO, 0:2 * D_CONV] = jnp.where(has_prev, cbuf[0:HALO, 0:2 * D_CONV], 0.0)
        if hi == tb + 2 * HALO:
            cbuf[hi - HALO:hi, 0:2 * D_CONV] = jnp.where(has_next, cbuf[hi - HALO:hi, 0:2 * D_CONV], 0.0)

    def q_outputs(p, hb, slabs):
        m_lo, m_hi = main[p]
        s0, s1 = slabs
        uq = _dot(main_rows(p, hb), w_ref[:, OFF_Q + s0 * LANES:OFF_Q + s1 * LANES])
        for s in range(s0, s1):
            qs = norm_rope(uq[:, (s - s0) * LANES:(s - s0 + 1) * LANES], gq_ref[...], LOG2E * HEAD_DIM ** -0.5,
                           cos_ref[m_lo:m_hi, :], sin_ref[m_lo:m_hi, :])
            q_ref[0, m_lo:m_hi, s * LANES:(s + 1) * LANES] = qs.astype(BF16)

    def kv_outputs(p, hb):
        m_lo, m_hi = main[p]
        ukv = _dot(main_rows(p, hb), w_ref[:, OFF_K:OFF_C])
        kn = norm_rope(ukv[:, 0:LANES], gk_ref[...], 1.0, cos_ref[m_lo:m_hi, :], sin_ref[m_lo:m_hi, :])
        for j, t in enumerate((kn, ukv[:, LANES:2 * LANES])):
            swapped = pltpu.roll(t, HEAD_DIM, 1)
            kv_ref[0, m_lo:m_hi, (2 * j) * LANES:(2 * j + 1) * LANES] = jnp.where(low_head, t, swapped).astype(BF16)
            kv_ref[0, m_lo:m_hi, (2 * j + 1) * LANES:(2 * j + 2) * LANES] = (
                jnp.where(low_head, swapped, t).astype(BF16))

    for p, (lo, hi) in enumerate(pieces):
        hb = normed(p)
        stage_short(p, hb)
        stage_glu(p, hb)
        q_outputs(p, hb, (0, D_ATTN // LANES))
        kv_outputs(p, hb)
        for c in range(tb // CONV_ROWS):
            if lo < (c + 1) * CONV_ROWS + 2 * HALO <= hi:
                conv_chunk(c)


def _proj(x, mods, mod_row, norm_g, w_in_b, layer, rot_row, rot_col, gq, gk, conv_a_w, conv_c_w, conv_c_b, ln_g,
          ln_b, tb):
    b, n, _ = x.shape
    const = lambda bb, i: (0, 0)
    hb = tb // HALO
    kern = functools.partial(_proj_kernel, tb, n)
    return pl.pallas_call(
        kern,
        grid=(b, n // tb),
        in_specs=[
            pl.BlockSpec((1, tb, D_MODEL), lambda bb, i: (bb, i, 0)),
            pl.BlockSpec((1, HALO, D_MODEL), lambda bb, i: (bb, jnp.maximum(i * hb - 1, 0), 0)),
            pl.BlockSpec((1, HALO, D_MODEL), lambda bb, i: (bb, jnp.minimum((i + 1) * hb, n // HALO - 1), 0)),
            pl.BlockSpec((1, 1, D_MODEL), lambda bb, i: (mod_row(bb), 0, 0)),
            pl.BlockSpec((1, 1, D_MODEL), lambda bb, i: (mod_row(bb), 0, 1)),
            pl.BlockSpec((1, D_MODEL), const),
            pl.BlockSpec((None, D_MODEL, D_IN), lambda bb, i: (layer, 0, 0)),
            pl.BlockSpec((2, tb // GRID_W, LANES), lambda bb, i: (0, i, 0)),
            pl.BlockSpec((2, GRID_W, LANES), lambda bb, i: (0, 0, 0)),
            pl.BlockSpec((1, LANES), const),
            pl.BlockSpec((1, LANES), const),
            pl.BlockSpec((SHORT_W, 8, D_CONV), lambda bb, i: (0, 0, 0)),
            pl.BlockSpec((CONF_W, 8, D_CONV), lambda bb, i: (0, 0, 0)),
            pl.BlockSpec((1, D_CONV), const),
            pl.BlockSpec((1, D_CONV), const),
            pl.BlockSpec((1, D_CONV), const),
        ],
        out_specs=[
            pl.BlockSpec((1, tb, 2 * D_CONV), lambda bb, i: (bb, i, 0)),
            pl.BlockSpec((1, tb, D_ATTN), lambda bb, i: (bb, i, 0)),
            pl.BlockSpec((1, tb, 4 * LANES), lambda bb, i: (bb, i, 0)),
        ],
        out_shape=[
            jax.ShapeDtypeStruct((b, n, 2 * D_CONV), BF16),
            jax.ShapeDtypeStruct((b, n, D_ATTN), BF16),
            jax.ShapeDtypeStruct((b, n, 4 * LANES), BF16),
        ],
        scratch_shapes=[pltpu.VMEM((tb + 2 * HALO, 3 * D_CONV), F32),
                        pltpu.VMEM((tb, LANES), F32), pltpu.VMEM((tb, LANES), F32)],
        compiler_params=pltpu.CompilerParams(
            dimension_semantics=("parallel", "parallel"), vmem_limit_bytes=VMEM_LIMIT),
        name="proj",
    )(x, x, x, mods, mods, norm_g, w_in_b, rot_row, rot_col, gq, gk, conv_a_w, conv_c_w, conv_c_b, ln_g, ln_b)


def _mix_kernel(local, tb, seq,
                x_ref, y_ref, q_ref, kv_ref, kvp_ref, kvn_ref, ckv_ref, sink_ref, wo_ref, g1_ref,
                o_ref, kvbuf, ybuf):
    i = pl.program_id(1)
    lane = lax.broadcasted_iota(jnp.int32, (1, LANES), 1)
    low_head = lane < HEAD_DIM
    if local:
        kvbuf[0:WINDOW, :] = kvp_ref[0]
        kvbuf[WINDOW:WINDOW + tb, :] = kv_ref[0]
        kvbuf[WINDOW + tb:WINDOW + tb + WINDOW, :] = kvn_ref[0]
        n_loc = 3 * WINDOW
        rel = (lax.broadcasted_iota(jnp.int32, (WINDOW, WINDOW), 1)
               - lax.broadcasted_iota(jnp.int32, (WINDOW, WINDOW), 0))

    def pair_values(blocks):
        rows = []
        for keep_low in (True, False):
            ones = (jnp.where(low_head, 1.0, 0.0) if keep_low else jnp.where(low_head, 0.0, 1.0)).astype(BF16)
            for v in blocks:
                z = jnp.zeros_like(v)
                masked = jnp.where(low_head, v, z) if keep_low else jnp.where(low_head, z, v)
                rows.append(jnp.concatenate([masked, jnp.broadcast_to(ones, v.shape)], axis=1))
        return jnp.concatenate(rows, axis=0)

    def project(j):
        j0 = pl.multiple_of(j * WINDOW, WINDOW)
        out = _dot(ybuf[pl.ds(j0, WINDOW), :], wo_ref[...])
        o_ref[0, pl.ds(j0, WINDOW), :] = x_ref[0, pl.ds(j0, WINDOW), :] + g1_ref[0] * out

    def block_scores(j, kvh):
        j0 = j * WINDOW
        qj = q_ref[0, j0:j0 + WINDOW, :]
        parts = []
        for g in range(GROUP):
            h = kvh * GROUP + g
            slab = qj[:, (h // 2) * LANES:(h // 2 + 1) * LANES]
            parts.append(jnp.where(low_head if h % 2 == 0 else jnp.logical_not(low_head),
                                   slab, jnp.zeros_like(slab)))
        qs = jnp.concatenate(parts, axis=0)
        kc = ckv_ref[0, :, kvh * LANES:(kvh + 1) * LANES]
        vc = ckv_ref[0, :, (N_KV + kvh) * LANES:(N_KV + kvh + 1) * LANES]
        s_ctx = _dot_nt(qs, kc)
        s_loc = None
        if local:
            kw = kvbuf[j0:j0 + n_loc, kvh * LANES:(kvh + 1) * LANES]
            vw = kvbuf[j0:j0 + n_loc, (N_KV + kvh) * LANES:(N_KV + kvh + 1) * LANES]
            s_loc = _dot_nt(qs, kw)
            return s_ctx, s_loc, pair_values((vw, vc))
        return s_ctx, s_loc, pair_values((vc,))

    def mixer_rows(j, scores, fillers):
        j0 = j * WINDOW
        yj = y_ref[0, j0:j0 + WINDOW, :]
        y_parts = [yj[:, 0:D_CONV]]
        if local:
            base = i * tb + j0
            big = jnp.int32(1 << 20)
            prev_ok = rel >= jnp.where(base >= WINDOW, 0, big)
            next_ok = rel <= jnp.where(base + 2 * WINDOW <= seq, 0, -big)
        if 0 in fillers:
            fillers[0]()
        n_pv = 0
        for kvh in range(N_KV):
            s_ctx, s_loc, v_pair = scores[kvh]
            for pair in range(GROUP // 2):
                probs, sink_terms = [], []
                for half in range(2):
                    g = 2 * pair + half
                    sk = sink_ref[kvh * GROUP + g] * LOG2E
                    sc = s_ctx[g * WINDOW:(g + 1) * WINDOW]
                    if local:
                        sg = s_loc[g * WINDOW:(g + 1) * WINDOW]
                        sl = jnp.concatenate([
                            jnp.where(prev_ok, sg[:, 0:WINDOW], NEG_INF),
                            sg[:, WINDOW:2 * WINDOW],
                            jnp.where(next_ok, sg[:, 2 * WINDOW:3 * WINDOW], NEG_INF)], axis=1)
                        m = jnp.maximum(jnp.maximum(jnp.max(sl, axis=-1, keepdims=True),
                                                    jnp.max(sc, axis=-1, keepdims=True)), sk)
                        probs.append(jnp.exp2(sl - m).astype(BF16))
                    else:
                        m = jnp.maximum(jnp.max(sc, axis=-1, keepdims=True), sk)
                    probs.append(jnp.exp2(sc - m).astype(BF16))
                    sink_terms.append(jnp.exp2(sk - m))
                o = _dot(jnp.concatenate(probs, axis=1), v_pair)
                n_pv += 1
                if n_pv in fillers:
                    fillers[n_pv]()
                den = o[:, LANES:2 * LANES] + jnp.where(low_head, sink_terms[0], sink_terms[1])
                y_parts.append((o[:, 0:LANES] / den).astype(BF16))
        y_parts.append(yj[:, D_CONV:2 * D_CONV])
        return jnp.concatenate(y_parts, axis=1)

    n_blocks = tb // WINDOW
    scores = [block_scores(0, kvh) for kvh in range(N_KV)]
    for j in range(n_blocks):
        nxt = []
        fillers = {}
        if j > 0:
            fillers[0] = functools.partial(project, j - 1)
        if j + 1 < n_blocks:
            fillers[GROUP // 2] = lambda j=j, nxt=nxt: nxt.append(block_scores(j + 1, 0))
            fillers[GROUP] = lambda j=j, nxt=nxt: nxt.append(block_scores(j + 1, 1))
        ybuf[j * WINDOW:(j + 1) * WINDOW, :] = mixer_rows(j, scores, fillers)
        scores = nxt
    project(n_blocks - 1)


def _mix(x, y, q, kv, ckv, mods, mod_row, sink, w_out_b, layer, tb, local):
    b, n, _ = x.shape
    n_ctx = ckv.shape[1]
    const = lambda bb, i: (0, 0)
    wb = tb // WINDOW
    kern = functools.partial(_mix_kernel, local, tb, n)
    return pl.pallas_call(
        kern,
        grid=(b, n // tb),
        in_specs=[
            pl.BlockSpec((1, tb, D_MODEL), lambda bb, i: (bb, i, 0)),
            pl.BlockSpec((1, tb, 2 * D_CONV), lambda bb, i: (bb, i, 0)),
            pl.BlockSpec((1, tb, D_ATTN), lambda bb, i: (bb, i, 0)),
            pl.BlockSpec((1, tb, 4 * LANES), lambda bb, i: (bb, i, 0)),
            pl.BlockSpec((1, WINDOW, 4 * LANES), lambda bb, i: (bb, jnp.maximum(i * wb - 1, 0), 0)),
            pl.BlockSpec((1, WINDOW, 4 * LANES),
                         lambda bb, i: (bb, jnp.minimum((i + 1) * wb, n // WINDOW - 1), 0)),
            pl.BlockSpec((1, n_ctx, 4 * LANES), lambda bb, i: (bb, 0, 0)),
            pl.BlockSpec(memory_space=pltpu.SMEM),
            pl.BlockSpec((None, D_MODEL, D_MODEL), lambda bb, i: (layer, 0, 0)),
            pl.BlockSpec((1, 1, D_MODEL), lambda bb, i: (mod_row(bb), 0, 2)),
        ],
        out_specs=pl.BlockSpec((1, tb, D_MODEL), lambda bb, i: (bb, i, 0)),
        out_shape=jax.ShapeDtypeStruct((b, n, D_MODEL), F32),
        scratch_shapes=[
            pltpu.VMEM((tb + 2 * WINDOW, 4 * LANES), BF16),
            pltpu.VMEM((tb, D_MODEL), BF16),
        ],
        compiler_params=pltpu.CompilerParams(
            dimension_semantics=("parallel", "parallel"), vmem_limit_bytes=VMEM_LIMIT),
        name="mix_local" if local else "mix_ctx",
    )(x, y, q, kv, kv, kv, ckv, sink, w_out_b, mods)


def _mlp_kernel(x_ref, sh_ref, sc_ref, gate_ref, g_ref, w1_ref, w2_ref, o_ref):
    x = x_ref[0]
    ms = jnp.mean(x * x, axis=-1, keepdims=True)
    h = (x * lax.rsqrt(ms + EPS)) * g_ref[...]
    h = h * (1.0 + sc_ref[0]) + sh_ref[0]
    hb = h.astype(BF16)
    acc = None
    for c in range(D_FF // FF_CHUNK):
        hid = _dot(hb, w1_ref[:, c * FF_CHUNK:(c + 1) * FF_CHUNK])
        r = jnp.maximum(hid, 0.0)
        part = _dot((r * r).astype(BF16), w2_ref[c * FF_CHUNK:(c + 1) * FF_CHUNK, :])
        acc = part if acc is None else acc + part
    o_ref[0] = x + gate_ref[0] * acc


def _mlp(x, mods, mod_row, norm_g, w1_b, w2_b, layer, tb):
    b, n, _ = x.shape
    const = lambda bb, i: (0, 0)
    return pl.pallas_call(
        _mlp_kernel,
        grid=(b, n // tb),
        in_specs=[
            pl.BlockSpec((1, tb, D_MODEL), lambda bb, i: (bb, i, 0)),
            pl.BlockSpec((1, 1, D_MODEL), lambda bb, i: (mod_row(bb), 0, 3)),
            pl.BlockSpec((1, 1, D_MODEL), lambda bb, i: (mod_row(bb), 0, 4)),
            pl.BlockSpec((1, 1, D_MODEL), lambda bb, i: (mod_row(bb), 0, 5)),
            pl.BlockSpec((1, D_MODEL), const),
            pl.BlockSpec((None, D_MODEL, D_FF), lambda bb, i: (layer, 0, 0)),
            pl.BlockSpec((None, D_FF, D_MODEL), lambda bb, i: (layer, 0, 0)),
        ],
        out_specs=pl.BlockSpec((1, tb, D_MODEL), lambda bb, i: (bb, i, 0)),
        out_shape=jax.ShapeDtypeStruct((b, n, D_MODEL), F32),
        compiler_params=pltpu.CompilerParams(
            dimension_semantics=("parallel", "parallel"), vmem_limit_bytes=VMEM_LIMIT),
        name="mlp",
    )(x, mods, mods, mods, norm_g, w1_b, w2_b)


def _rope_tables(seq):
    d_axis = HEAD_DIM // 2
    inv_freq = ROPE_BASE ** (-jnp.arange(0, d_axis, 2, dtype=F32) / d_axis)
    reps = LANES // HEAD_DIM

    def table(n_pos, first):
        ang = jnp.arange(n_pos, dtype=jnp.int32).astype(F32)[:, None] * inv_freq[None, :]
        zero = jnp.zeros_like(ang)
        cos = [jnp.cos(ang), jnp.cos(ang), zero, zero]
        sin = [-jnp.sin(ang), jnp.sin(ang), zero, zero]
        if not first:
            cos, sin = cos[2:] + cos[:2], sin[2:] + sin[:2]
        return jnp.stack([jnp.tile(jnp.concatenate(cos, axis=-1), (1, reps)),
                          jnp.tile(jnp.concatenate(sin, axis=-1), (1, reps))])

    return table(seq // GRID_W, True), table(GRID_W, False)


def kernel(x, c, ctx, c_ctx, w_mod, b_mod, norm1_g, w_in, conv_a_w, q_norm_g, k_norm_g, attn_sink,
           conv_c_w, conv_c_b, ln_c_g, ln_c_b, w_out, norm2_g, w_mlp1, w_mlp2):
    batch, seq, _ = x.shape
    n_ctx = ctx.shape[1]
    depth = w_mod.shape[0]
    tb = min(TOKEN_BLOCK, seq)
    tbw = min(WIDE_BLOCK, seq)
    tb_ctx = n_ctx
    ctx_row = batch

    cvec = jnp.concatenate([c, c_ctx[None, :], jnp.zeros((8 - batch - 1, D_MODEL), F32)], axis=0)
    mods_all = _modulation(cvec.T, batch + 1, w_mod, b_mod).reshape(depth, 8, 1, 6 * D_MODEL)

    rot_row, rot_col = _rope_tables(seq)
    rot_row_c = jnp.stack([jnp.ones((n_ctx // GRID_W, LANES), F32), jnp.zeros((n_ctx // GRID_W, LANES), F32)])
    rot_col_c = jnp.stack([jnp.ones((GRID_W, LANES), F32), jnp.zeros((GRID_W, LANES), F32)])
    reps = LANES // HEAD_DIM
    lat_row = lambda bb: bb
    cx_row = lambda bb: ctx_row

    w_in_b = w_in.astype(BF16)
    w_out_b = w_out.astype(BF16)
    w1_b = w_mlp1.astype(BF16)
    w2_b = w_mlp2.astype(BF16)

    xc = ctx
    for l in range(depth):
        last = l == depth - 1
        mods = mods_all[l]
        g1n = norm1_g[l][None, :]
        g2n = norm2_g[l][None, :]
        gq = jnp.tile(q_norm_g[l], reps)[None, :]
        gk = jnp.tile(k_norm_g[l], reps)[None, :]
        conv_args = (jnp.broadcast_to(conv_a_w[l][:, None, :], (SHORT_W, 8, D_CONV)),
                     jnp.broadcast_to(conv_c_w[l][:, None, :], (CONF_W, 8, D_CONV)),
                     conv_c_b[l][None, :], ln_c_g[l][None, :], ln_c_b[l][None, :])

        y_c, q_c, kv_c = _proj(xc, mods, cx_row, g1n, w_in_b, l, rot_row_c, rot_col_c, gq, gk, *conv_args, tb_ctx)
        y, q, kv = _proj(x, mods, lat_row, g1n, w_in_b, l, rot_row, rot_col, gq, gk, *conv_args, tbw)
        x = _mix(x, y, q, kv, kv_c, mods, lat_row, attn_sink[l], w_out_b, l, tbw, True)
        x = _mlp(x, mods, lat_row, g2n, w1_b, w2_b, l, tb)
        if not last:
            xc = _mix(xc, y_c, q_c, kv_c, kv_c, mods, cx_row, attn_sink[l], w_out_b, l, tb_ctx, False)
            xc = _mlp(xc, mods, cx_row, g2n, w1_b, w2_b, l, tb_ctx)
    return x
```

```python
import functools

import jax
import jax.numpy as jnp
from jax import lax
from jax.experimental import pallas as pl
from jax.experimental.pallas import tpu as pltpu

F32 = jnp.float32
BF16 = jnp.bfloat16

D_MODEL = 1024
HEAD_DIM = 64
N_HEADS = 8
N_KV = 2
GROUP = N_HEADS // N_KV
D_CONV = 256
D_ATTN = N_HEADS * HEAD_DIM
D_FF = 4 * D_MODEL
GRID_W = 64
WINDOW = 128
SHORT_W = 3
CONF_W = 31
HALO = 16
ROPE_BASE = 10000.0
EPS = 1e-6
NEG_INF = -1e30
LOG2E = 1.4426950408889634

OFF_Q = 3 * D_CONV
OFF_K = OFF_Q + D_ATTN
OFF_C = OFF_K + 2 * N_KV * HEAD_DIM
D_IN = OFF_C + 2 * D_CONV

LANES = 128
SUBLANES = 8
TOKEN_BLOCK = 512
WIDE_BLOCK = 1024
CONV_ROWS = 64
FF_CHUNK = 1024
MOD_COLS = 1536
VMEM_LIMIT = 56 * 1024 * 1024


def _sigmoid(t):
    return 1.0 / (1.0 + jnp.exp(-t))


def _dot(a, b):
    return jnp.dot(a, b, preferred_element_type=F32)


def _dot_nt(a, b):
    return lax.dot_general(a, b, (((1,), (1,)), ((), ())), preferred_element_type=F32)


def _mod_kernel(n_vec, c_ref, w_ref, b_ref, o_ref):
    c = c_ref[...]
    s = c * _sigmoid(c)
    w = w_ref[0]
    rows = [jnp.sum(w * s[:, r:r + 1], axis=0, keepdims=True) for r in range(n_vec)]
    rows.append(jnp.zeros((SUBLANES - n_vec, w.shape[1]), F32))
    o_ref[0] = jnp.concatenate(rows, axis=0) + b_ref[0]


def _modulation(cvec_t, n_vec, w_mod, b_mod):
    depth, _, n = w_mod.shape
    return pl.pallas_call(
        functools.partial(_mod_kernel, n_vec),
        grid=(depth, n // MOD_COLS),
        in_specs=[
            pl.BlockSpec((D_MODEL, SUBLANES), lambda l, j: (0, 0)),
            pl.BlockSpec((1, D_MODEL, MOD_COLS), lambda l, j: (l, 0, j)),
            pl.BlockSpec((1, 1, MOD_COLS), lambda l, j: (l, 0, j)),
        ],
        out_specs=pl.BlockSpec((1, SUBLANES, MOD_COLS), lambda l, j: (l, 0, j)),
        out_shape=jax.ShapeDtypeStruct((depth, SUBLANES, n), F32),
        compiler_params=pltpu.CompilerParams(
            dimension_semantics=("arbitrary", "arbitrary"), vmem_limit_bytes=VMEM_LIMIT),
        name="modulation",
    )(cvec_t, w_mod, b_mod.reshape(depth, 1, n))


def _phase_conv(buf, col0, w_ref, n_taps, r0, rows):
    phases = {}
    for k in range(n_taps):
        off = HALO + k - n_taps // 2
        phases.setdefault(off % SUBLANES, []).append((off - off % SUBLANES, k))
    out = None
    for p in sorted(phases):
        n_rows = rows if p == 0 else rows + SUBLANES
        z = None
        for aligned, k in phases[p]:
            xs = buf[r0 + aligned:r0 + aligned + n_rows, col0:col0 + D_CONV]
            term = (xs.reshape(n_rows // SUBLANES, SUBLANES, D_CONV) * w_ref[k]).reshape(n_rows, D_CONV)
            z = term if z is None else z + term
        z = z[p:p + rows]
        out = z if out is None else out + z
    return out


def _proj_kernel(tb, seq,
                 x_ref, xp_ref, xn_ref, sh_ref, sc_ref, g_ref, w_ref, rot_row_ref, rot_col_ref, gq_ref, gk_ref,
                 wa_ref, wc_ref, cb_ref, lg_ref, lb_ref,
                 y_ref, q_ref, kv_ref, cbuf, cos_ref, sin_ref):
    i = pl.program_id(1)
    has_prev = i > 0
    has_next = i < seq // tb - 1

    by_row = (lax.broadcasted_iota(jnp.int32, (1, LANES), 1) % HEAD_DIM) < HEAD_DIM // 2
    for r in range(tb // GRID_W):
        cos_ref[r * GRID_W:(r + 1) * GRID_W, :] = jnp.where(by_row, rot_row_ref[0, r:r + 1, :], rot_col_ref[0])
        sin_ref[r * GRID_W:(r + 1) * GRID_W, :] = jnp.where(by_row, rot_row_ref[1, r:r + 1, :], rot_col_ref[1])

    gain = g_ref[...] * (1.0 + sc_ref[0])

    def norm_mod(x):
        ms = jnp.mean(x * x, axis=-1, keepdims=True)
        return ((x * lax.rsqrt(ms + EPS)) * gain + sh_ref[0]).astype(BF16)

    lane = lax.broadcasted_iota(jnp.int32, (1, LANES), 1)
    rot_from_above = (lane % (HEAD_DIM // 2)) < (HEAD_DIM // 4)
    low_head = lane < HEAD_DIM
    r_i = lax.broadcasted_iota(jnp.int32, (LANES, LANES), 0) // HEAD_DIM
    c_i = lax.broadcasted_iota(jnp.int32, (LANES, LANES), 1) // HEAD_DIM
    head_ones = jnp.where(r_i == c_i, 1.0, 0.0).astype(BF16)

    def norm_rope(t, gain, scale, cos, sin):
        ssq = _dot((t * t).astype(BF16), head_ones)
        rs = lax.rsqrt(ssq * (1.0 / HEAD_DIM) + EPS) * scale
        tg = t * gain
        partner = jnp.where(rot_from_above, pltpu.roll(tg, LANES - HEAD_DIM // 4, 1),
                            pltpu.roll(tg, HEAD_DIM // 4, 1))
        return rs * (tg * cos + partner * sin)

    def conv_chunk(c):
        r0 = c * CONV_ROWS
        a = _phase_conv(cbuf, 0, wa_ref, SHORT_W, r0, CONV_ROWS)
        b_gate = cbuf[HALO + r0:HALO + r0 + CONV_ROWS, 2 * D_CONV:3 * D_CONV]
        y_ref[0, r0:r0 + CONV_ROWS, 0:D_CONV] = (b_gate * a).astype(BF16)
        acc = _phase_conv(cbuf, D_CONV, wc_ref, CONF_W, r0, CONV_ROWS) + cb_ref[...]
        mu = jnp.mean(acc, axis=-1, keepdims=True)
        xc = acc - mu
        var = jnp.mean(xc * xc, axis=-1, keepdims=True)
        yn = (xc * lax.rsqrt(var + EPS)) * lg_ref[...] + lb_ref[...]
        y_ref[0, r0:r0 + CONV_ROWS, D_CONV:2 * D_CONV] = (yn * _sigmoid(yn)).astype(BF16)

    half = tb // 2
    pieces = ((0, half + 2 * HALO), (half + 2 * HALO, tb + 2 * HALO))
    main = tuple((max(lo - HALO, 0), min(hi - HALO, tb)) for lo, hi in pieces)

    def normed(p):
        lo, hi = pieces[p]
        parts = [norm_mod(x_ref[0, main[p][0]:main[p][1], :])]
        if lo == 0:
            parts.insert(0, norm_mod(xp_ref[0]))
        if hi == tb + 2 * HALO:
            parts.append(norm_mod(xn_ref[0]))
        return jnp.concatenate(parts, axis=0)

    def main_rows(p, hb):
        return hb[main[p][0] + HALO - pieces[p][0]:main[p][1] + HALO - pieces[p][0]]

    def stage_short(p, hb):
        lo, hi = pieces[p]
        ua = _dot(hb, w_ref[:, 0:OFF_Q])
        cbuf[lo:hi, 0:D_CONV] = ua[:, 2 * D_CONV:3 * D_CONV] * ua[:, 0:D_CONV]
        cbuf[lo:hi, 2 * D_CONV:3 * D_CONV] = ua[:, D_CONV:2 * D_CONV]

    def stage_glu(p, hb):
        lo, hi = pieces[p]
        uc = _dot(hb, w_ref[:, OFF_C:D_IN])
        cbuf[lo:hi, D_CONV:2 * D_CONV] = uc[:, 0:D_CONV] * _sigmoid(uc[:, D_CONV:2 * D_CONV])
        if lo == 0:
            cbuf[0:HALO, 0:2 * D_CONV] = jnp.where(has_prev, cbuf[0:HALO, 0:2 * D_CONV], 0.0)
        if hi == tb + 2 * HALO:
            cbuf[hi - HALO:hi, 0:2 * D_CONV] = jnp.where(has_next, cbuf[hi - HALO:hi, 0:2 * D_CONV], 0.0)

    def q_outputs(p, hb, slabs):
        m_lo, m_hi = main[p]
        s0, s1 = slabs
        uq = _dot(main_rows(p, hb), w_ref[:, OFF_Q + s0 * LANES:OFF_Q + s1 * LANES])
        for s in range(s0, s1):
            qs = norm_rope(uq[:, (s - s0) * LANES:(s - s0 + 1) * LANES], gq_ref[...], LOG2E * HEAD_DIM ** -0.5,
                           cos_ref[m_lo:m_hi, :], sin_ref[m_lo:m_hi, :])
            q_ref[0, m_lo:m_hi, s * LANES:(s + 1) * LANES] = qs.astype(BF16)

    def kv_outputs(p, hb):
        m_lo, m_hi = main[p]
        ukv = _dot(main_rows(p, hb), w_ref[:, OFF_K:OFF_C])
        kn = norm_rope(ukv[:, 0:LANES], gk_ref[...], 1.0, cos_ref[m_lo:m_hi, :], sin_ref[m_lo:m_hi, :])
        for j, t in enumerate((kn, ukv[:, LANES:2 * LANES])):
            swapped = pltpu.roll(t, HEAD_DIM, 1)
            kv_ref[0, m_lo:m_hi, (2 * j) * LANES:(2 * j + 1) * LANES] = jnp.where(low_head, t, swapped).astype(BF16)
            kv_ref[0, m_lo:m_hi, (2 * j + 1) * LANES:(2 * j + 2) * LANES] = (
                jnp.where(low_head, swapped, t).astype(BF16))

    for p, (lo, hi) in enumerate(pieces):
        hb = normed(p)
        stage_short(p, hb)
        stage_glu(p, hb)
        q_outputs(p, hb, (0, D_ATTN // LANES))
        kv_outputs(p, hb)
        for c in range(tb // CONV_ROWS):
            if lo < (c + 1) * CONV_ROWS + 2 * HALO <= hi:
                conv_chunk(c)


def _proj(x, mods, mod_row, norm_g, w_in_b, layer, rot_row, rot_col, gq, gk, conv_a_w, conv_c_w, conv_c_b, ln_g,
          ln_b, tb):
    b, n, _ = x.shape
    const = lambda bb, i: (0, 0)
    hb = tb // HALO
    kern = functools.partial(_proj_kernel, tb, n)
    return pl.pallas_call(
        kern,
        grid=(b, n // tb),
        in_specs=[
            pl.BlockSpec((1, tb, D_MODEL), lambda bb, i: (bb, i, 0)),
            pl.BlockSpec((1, HALO, D_MODEL), lambda bb, i: (bb, jnp.maximum(i * hb - 1, 0), 0)),
            pl.BlockSpec((1, HALO, D_MODEL), lambda bb, i: (bb, jnp.minimum((i + 1) * hb, n // HALO - 1), 0)),
            pl.BlockSpec((1, 1, D_MODEL), lambda bb, i: (mod_row(bb), 0, 0)),
            pl.BlockSpec((1, 1, D_MODEL), lambda bb, i: (mod_row(bb), 0, 1)),
            pl.BlockSpec((1, D_MODEL), const),
            pl.BlockSpec((None, D_MODEL, D_IN), lambda bb, i: (layer, 0, 0)),
            pl.BlockSpec((2, tb // GRID_W, LANES), lambda bb, i: (0, i, 0)),
            pl.BlockSpec((2, GRID_W, LANES), lambda bb, i: (0, 0, 0)),
            pl.BlockSpec((1, LANES), const),
            pl.BlockSpec((1, LANES), const),
            pl.BlockSpec((SHORT_W, SUBLANES, D_CONV), lambda bb, i: (0, 0, 0)),
            pl.BlockSpec((CONF_W, SUBLANES, D_CONV), lambda bb, i: (0, 0, 0)),
            pl.BlockSpec((1, D_CONV), const),
            pl.BlockSpec((1, D_CONV), const),
            pl.BlockSpec((1, D_CONV), const),
        ],
        out_specs=[
            pl.BlockSpec((1, tb, 2 * D_CONV), lambda bb, i: (bb, i, 0)),
            pl.BlockSpec((1, tb, D_ATTN), lambda bb, i: (bb, i, 0)),
            pl.BlockSpec((1, tb, 4 * LANES), lambda bb, i: (bb, i, 0)),
        ],
        out_shape=[
            jax.ShapeDtypeStruct((b, n, 2 * D_CONV), BF16),
            jax.ShapeDtypeStruct((b, n, D_ATTN), BF16),
            jax.ShapeDtypeStruct((b, n, 4 * LANES), BF16),
        ],
        scratch_shapes=[pltpu.VMEM((tb + 2 * HALO, 3 * D_CONV), F32),
                        pltpu.VMEM((tb, LANES), F32), pltpu.VMEM((tb, LANES), F32)],
        compiler_params=pltpu.CompilerParams(
            dimension_semantics=("parallel", "parallel"), vmem_limit_bytes=VMEM_LIMIT),
        name="proj",
    )(x, x, x, mods, mods, norm_g, w_in_b, rot_row, rot_col, gq, gk, conv_a_w, conv_c_w, conv_c_b, ln_g, ln_b)


def _mix_kernel(local, tb, seq,
                x_ref, y_ref, q_ref, kv_ref, kvp_ref, kvn_ref, ckv_ref, sink_ref, wo_ref, g1_ref,
                o_ref, kvbuf, ybuf):
    i = pl.program_id(1)
    lane = lax.broadcasted_iota(jnp.int32, (1, LANES), 1)
    low_head = lane < HEAD_DIM
    if local:
        kvbuf[0:WINDOW, :] = kvp_ref[0]
        kvbuf[WINDOW:WINDOW + tb, :] = kv_ref[0]
        kvbuf[WINDOW + tb:WINDOW + tb + WINDOW, :] = kvn_ref[0]
        n_loc = 3 * WINDOW
        rel = (lax.broadcasted_iota(jnp.int32, (WINDOW, WINDOW), 1)
               - lax.broadcasted_iota(jnp.int32, (WINDOW, WINDOW), 0))

    def pair_values(blocks):
        rows = []
        for keep_low in (True, False):
            ones = (jnp.where(low_head, 1.0, 0.0) if keep_low else jnp.where(low_head, 0.0, 1.0)).astype(BF16)
            for v in blocks:
                z = jnp.zeros_like(v)
                masked = jnp.where(low_head, v, z) if keep_low else jnp.where(low_head, z, v)
                rows.append(jnp.concatenate([masked, jnp.broadcast_to(ones, v.shape)], axis=1))
        return jnp.concatenate(rows, axis=0)

    def project(j):
        j0 = pl.multiple_of(j * WINDOW, WINDOW)
        out = _dot(ybuf[pl.ds(j0, WINDOW), :], wo_ref[...])
        o_ref[0, pl.ds(j0, WINDOW), :] = x_ref[0, pl.ds(j0, WINDOW), :] + g1_ref[0] * out

    def block_scores(j, kvh):
        j0 = j * WINDOW
        qj = q_ref[0, j0:j0 + WINDOW, :]
        parts = []
        for g in range(GROUP):
            h = kvh * GROUP + g
            slab = qj[:, (h // 2) * LANES:(h // 2 + 1) * LANES]
            parts.append(jnp.where(low_head if h % 2 == 0 else jnp.logical_not(low_head),
                                   slab, jnp.zeros_like(slab)))
        qs = jnp.concatenate(parts, axis=0)
        kc = ckv_ref[0, :, kvh * LANES:(kvh + 1) * LANES]
        vc = ckv_ref[0, :, (N_KV + kvh) * LANES:(N_KV + kvh + 1) * LANES]
        s_ctx = _dot_nt(qs, kc)
        s_loc = None
        if local:
            kw = kvbuf[j0:j0 + n_loc, kvh * LANES:(kvh + 1) * LANES]
            vw = kvbuf[j0:j0 + n_loc, (N_KV + kvh) * LANES:(N_KV + kvh + 1) * LANES]
            s_loc = _dot_nt(qs, kw)
            return s_ctx, s_loc, pair_values((vw, vc))
        return s_ctx, s_loc, pair_values((vc,))

    def mixer_rows(j, scores, fillers):
        j0 = j * WINDOW
        yj = y_ref[0, j0:j0 + WINDOW, :]
        y_parts = [yj[:, 0:D_CONV]]
        if local:
            base = i * tb + j0
            never = jnp.int32(2 * WINDOW)
            prev_ok = rel >= jnp.where(base >= WINDOW, 0, never)
            next_ok = rel <= jnp.where(base + 2 * WINDOW <= seq, 0, -never)
        if 0 in fillers:
            fillers[0]()
        n_pv = 0
        for kvh in range(N_KV):
            s_ctx, s_loc, v_pair = scores[kvh]
            for pair in range(GROUP // 2):
                probs, sink_terms = [], []
                for half in range(2):
                    g = 2 * pair + half
                    sk = sink_ref[kvh * GROUP + g] * LOG2E
                    sc = s_ctx[g * WINDOW:(g + 1) * WINDOW]
                    if local:
                        sg = s_loc[g * WINDOW:(g + 1) * WINDOW]
                        sl = jnp.concatenate([
                            jnp.where(prev_ok, sg[:, 0:WINDOW], NEG_INF),
                            sg[:, WINDOW:2 * WINDOW],
                            jnp.where(next_ok, sg[:, 2 * WINDOW:3 * WINDOW], NEG_INF)], axis=1)
                        m = jnp.maximum(jnp.maximum(jnp.max(sl, axis=-1, keepdims=True),
                                                    jnp.max(sc, axis=-1, keepdims=True)), sk)
                        probs.append(jnp.exp2(sl - m).astype(BF16))
                    else:
                        m = jnp.maximum(jnp.max(sc, axis=-1, keepdims=True), sk)
                    probs.append(jnp.exp2(sc - m).astype(BF16))
                    sink_terms.append(jnp.exp2(sk - m))
                o = _dot(jnp.concatenate(probs, axis=1), v_pair)
                n_pv += 1
                if n_pv in fillers:
                    fillers[n_pv]()
                den = o[:, LANES:2 * LANES] + jnp.where(low_head, sink_terms[0], sink_terms[1])
                y_parts.append((o[:, 0:LANES] / den).astype(BF16))
        y_parts.append(yj[:, D_CONV:2 * D_CONV])
        return jnp.concatenate(y_parts, axis=1)

    n_blocks = tb // WINDOW
    scores = [block_scores(0, kvh) for kvh in range(N_KV)]
    for j in range(n_blocks):
        nxt = []
        fillers = {}
        if j > 0:
            fillers[0] = functools.partial(project, j - 1)
        if j + 1 < n_blocks:
            fillers[GROUP // 2] = lambda j=j, nxt=nxt: nxt.append(block_scores(j + 1, 0))
            fillers[GROUP] = lambda j=j, nxt=nxt: nxt.append(block_scores(j + 1, 1))
        ybuf[j * WINDOW:(j + 1) * WINDOW, :] = mixer_rows(j, scores, fillers)
        scores = nxt
    project(n_blocks - 1)


def _mix(x, y, q, kv, ckv, mods, mod_row, sink, w_out_b, layer, tb, local):
    b, n, _ = x.shape
    n_ctx = ckv.shape[1]
    const = lambda bb, i: (0, 0)
    wb = tb // WINDOW
    kern = functools.partial(_mix_kernel, local, tb, n)
    return pl.pallas_call(
        kern,
        grid=(b, n // tb),
        in_specs=[
            pl.BlockSpec((1, tb, D_MODEL), lambda bb, i: (bb, i, 0)),
            pl.BlockSpec((1, tb, 2 * D_CONV), lambda bb, i: (bb, i, 0)),
            pl.BlockSpec((1, tb, D_ATTN), lambda bb, i: (bb, i, 0)),
            pl.BlockSpec((1, tb, 4 * LANES), lambda bb, i: (bb, i, 0)),
            pl.BlockSpec((1, WINDOW, 4 * LANES), lambda bb, i: (bb, jnp.maximum(i * wb - 1, 0), 0)),
            pl.BlockSpec((1, WINDOW, 4 * LANES),
                         lambda bb, i: (bb, jnp.minimum((i + 1) * wb, n // WINDOW - 1), 0)),
            pl.BlockSpec((1, n_ctx, 4 * LANES), lambda bb, i: (bb, 0, 0)),
            pl.BlockSpec(memory_space=pltpu.SMEM),
            pl.BlockSpec((None, D_MODEL, D_MODEL), lambda bb, i: (layer, 0, 0)),
            pl.BlockSpec((1, 1, D_MODEL), lambda bb, i: (mod_row(bb), 0, 2)),
        ],
        out_specs=pl.BlockSpec((1, tb, D_MODEL), lambda bb, i: (bb, i, 0)),
        out_shape=jax.ShapeDtypeStruct((b, n, D_MODEL), F32),
        scratch_shapes=[
            pltpu.VMEM((tb + 2 * WINDOW, 4 * LANES), BF16),
            pltpu.VMEM((tb, D_MODEL), BF16),
        ],
        compiler_params=pltpu.CompilerParams(
            dimension_semantics=("parallel", "parallel"), vmem_limit_bytes=VMEM_LIMIT),
        name="mix_local" if local else "mix_ctx",
    )(x, y, q, kv, kv, kv, ckv, sink, w_out_b, mods)


def _mlp_kernel(x_ref, sh_ref, sc_ref, gate_ref, g_ref, w1_ref, w2_ref, o_ref):
    x = x_ref[0]
    ms = jnp.mean(x * x, axis=-1, keepdims=True)
    h = (x * lax.rsqrt(ms + EPS)) * g_ref[...]
    h = h * (1.0 + sc_ref[0]) + sh_ref[0]
    hb = h.astype(BF16)
    acc = None
    for c in range(D_FF // FF_CHUNK):
        hid = _dot(hb, w1_ref[:, c * FF_CHUNK:(c + 1) * FF_CHUNK])
        r = jnp.maximum(hid, 0.0)
        part = _dot((r * r).astype(BF16), w2_ref[c * FF_CHUNK:(c + 1) * FF_CHUNK, :])
        acc = part if acc is None else acc + part
    o_ref[0] = x + gate_ref[0] * acc


def _mlp(x, mods, mod_row, norm_g, w1_b, w2_b, layer, tb):
    b, n, _ = x.shape
    const = lambda bb, i: (0, 0)
    return pl.pallas_call(
        _mlp_kernel,
        grid=(b, n // tb),
        in_specs=[
            pl.BlockSpec((1, tb, D_MODEL), lambda bb, i: (bb, i, 0)),
            pl.BlockSpec((1, 1, D_MODEL), lambda bb, i: (mod_row(bb), 0, 3)),
            pl.BlockSpec((1, 1, D_MODEL), lambda bb, i: (mod_row(bb), 0, 4)),
            pl.BlockSpec((1, 1, D_MODEL), lambda bb, i: (mod_row(bb), 0, 5)),
            pl.BlockSpec((1, D_MODEL), const),
            pl.BlockSpec((None, D_MODEL, D_FF), lambda bb, i: (layer, 0, 0)),
            pl.BlockSpec((None, D_FF, D_MODEL), lambda bb, i: (layer, 0, 0)),
        ],
        out_specs=pl.BlockSpec((1, tb, D_MODEL), lambda bb, i: (bb, i, 0)),
        out_shape=jax.ShapeDtypeStruct((b, n, D_MODEL), F32),
        compiler_params=pltpu.CompilerParams(
            dimension_semantics=("parallel", "parallel"), vmem_limit_bytes=VMEM_LIMIT),
        name="mlp",
    )(x, mods, mods, mods, norm_g, w1_b, w2_b)


def _rope_tables(seq):
    d_axis = HEAD_DIM // 2
    inv_freq = ROPE_BASE ** (-jnp.arange(0, d_axis, 2, dtype=F32) / d_axis)
    reps = LANES // HEAD_DIM

    def table(n_pos, first):
        ang = jnp.arange(n_pos, dtype=jnp.int32).astype(F32)[:, None] * inv_freq[None, :]
        zero = jnp.zeros_like(ang)
        cos = [jnp.cos(ang), jnp.cos(ang), zero, zero]
        sin = [-jnp.sin(ang), jnp.sin(ang), zero, zero]
        if not first:
            cos, sin = cos[2:] + cos[:2], sin[2:] + sin[:2]
        return jnp.stack([jnp.tile(jnp.concatenate(cos, axis=-1), (1, reps)),
                          jnp.tile(jnp.concatenate(sin, axis=-1), (1, reps))])

    return table(seq // GRID_W, True), table(GRID_W, False)


def kernel(x, c, ctx, c_ctx, w_mod, b_mod, norm1_g, w_in, conv_a_w, q_norm_g, k_norm_g, attn_sink,
           conv_c_w, conv_c_b, ln_c_g, ln_c_b, w_out, norm2_g, w_mlp1, w_mlp2):
    batch, seq, _ = x.shape
    n_ctx = ctx.shape[1]
    depth = w_mod.shape[0]
    tb = min(TOKEN_BLOCK, seq)
    tbw = min(WIDE_BLOCK, seq)
    tb_ctx = n_ctx
    ctx_row = batch
    assert x.shape[2] == D_MODEL and batch + 1 <= SUBLANES and HALO >= CONF_W // 2
    assert seq % tb == 0 and seq % tbw == 0 and tbw % GRID_W == 0 and tb % WINDOW == 0
    assert n_ctx % WINDOW == 0 and n_ctx % GRID_W == 0
    assert (tbw // 2) % CONV_ROWS == 0 and (tb_ctx // 2) % CONV_ROWS == 0

    cvec = jnp.concatenate([c, c_ctx[None, :], jnp.zeros((SUBLANES - batch - 1, D_MODEL), F32)], axis=0)
    mods_all = _modulation(cvec.T, batch + 1, w_mod, b_mod).reshape(depth, SUBLANES, 1, 6 * D_MODEL)

    rot_row, rot_col = _rope_tables(seq)
    rot_row_c = jnp.stack([jnp.ones((n_ctx // GRID_W, LANES), F32), jnp.zeros((n_ctx // GRID_W, LANES), F32)])
    rot_col_c = jnp.stack([jnp.ones((GRID_W, LANES), F32), jnp.zeros((GRID_W, LANES), F32)])
    reps = LANES // HEAD_DIM
    lat_row = lambda bb: bb
    cx_row = lambda bb: ctx_row

    w_in_b = w_in.astype(BF16)
    w_out_b = w_out.astype(BF16)
    w1_b = w_mlp1.astype(BF16)
    w2_b = w_mlp2.astype(BF16)

    xc = ctx
    for l in range(depth):
        last = l == depth - 1
        mods = mods_all[l]
        g1n = norm1_g[l][None, :]
        g2n = norm2_g[l][None, :]
        gq = jnp.tile(q_norm_g[l], reps)[None, :]
        gk = jnp.tile(k_norm_g[l], reps)[None, :]
        conv_args = (jnp.broadcast_to(conv_a_w[l][:, None, :], (SHORT_W, SUBLANES, D_CONV)),
                     jnp.broadcast_to(conv_c_w[l][:, None, :], (CONF_W, SUBLANES, D_CONV)),
                     conv_c_b[l][None, :], ln_c_g[l][None, :], ln_c_b[l][None, :])

        y_c, q_c, kv_c = _proj(xc, mods, cx_row, g1n, w_in_b, l, rot_row_c, rot_col_c, gq, gk, *conv_args, tb_ctx)
        y, q, kv = _proj(x, mods, lat_row, g1n, w_in_b, l, rot_row, rot_col, gq, gk, *conv_args, tbw)
        x = _mix(x, y, q, kv, kv_c, mods, lat_row, attn_sink[l], w_out_b, l, tbw, True)
        x = _mlp(x, mods, lat_row, g2n, w1_b, w2_b, l, tb)
        if not last:
            xc = _mix(xc, y_c, q_c, kv_c, kv_c, mods, cx_row, attn_sink[l], w_out_b, l, tb_ctx, False)
            xc = _mlp(xc, mods, cx_row, g2n, w1_b, w2_b, l, tb_ctx)
    return x
```

```python
import functools

import jax
import jax.numpy as jnp
from jax import lax
from jax.experimental import pallas as pl
from jax.experimental.pallas import tpu as pltpu

F32 = jnp.float32
BF16 = jnp.bfloat16

D_MODEL = 1024
HEAD_DIM = 64
N_HEADS = 8
N_KV = 2
GROUP = N_HEADS // N_KV
D_CONV = 256
D_ATTN = N_HEADS * HEAD_DIM
D_FF = 4 * D_MODEL
GRID_W = 64
WINDOW = 128
SHORT_W = 3
CONF_W = 31
HALO = 16
ROPE_BASE = 10000.0
EPS = 1e-6
NEG_INF = -1e30
LOG2E = 1.4426950408889634

OFF_Q = 3 * D_CONV
OFF_K = OFF_Q + D_ATTN
OFF_C = OFF_K + 2 * N_KV * HEAD_DIM
D_IN = OFF_C + 2 * D_CONV

LANES = 128
SUBLANES = 8
TOKEN_BLOCK = 1024
PROJ_BLOCK = 2048
CONV_ROWS = 64
FF_CHUNK = 2048
MOD_COLS = 1536
VMEM_LIMIT = 56 * 1024 * 1024


def _sigmoid(t):
    return 1.0 / (1.0 + jnp.exp(-t))


def _dot(a, b):
    return jnp.dot(a, b, preferred_element_type=F32)


def _dot_nt(a, b):
    return lax.dot_general(a, b, (((1,), (1,)), ((), ())), preferred_element_type=F32)


def _mod_kernel(n_vec, c_ref, w_ref, b_ref, o_ref):
    c = c_ref[...]
    s = c * _sigmoid(c)
    w = w_ref[0]
    rows = [jnp.sum(w * s[:, r:r + 1], axis=0, keepdims=True) for r in range(n_vec)]
    rows.append(jnp.zeros((SUBLANES - n_vec, w.shape[1]), F32))
    o_ref[0] = jnp.concatenate(rows, axis=0) + b_ref[0]


def _modulation(cvec_t, n_vec, w_mod, b_mod):
    depth, _, n = w_mod.shape
    return pl.pallas_call(
        functools.partial(_mod_kernel, n_vec),
        grid=(depth, n // MOD_COLS),
        in_specs=[
            pl.BlockSpec((D_MODEL, SUBLANES), lambda l, j: (0, 0)),
            pl.BlockSpec((1, D_MODEL, MOD_COLS), lambda l, j: (l, 0, j)),
            pl.BlockSpec((1, 1, MOD_COLS), lambda l, j: (l, 0, j)),
        ],
        out_specs=pl.BlockSpec((1, SUBLANES, MOD_COLS), lambda l, j: (l, 0, j)),
        out_shape=jax.ShapeDtypeStruct((depth, SUBLANES, n), F32),
        compiler_params=pltpu.CompilerParams(
            dimension_semantics=("arbitrary", "arbitrary"), vmem_limit_bytes=VMEM_LIMIT),
        name="modulation",
    )(cvec_t, w_mod, b_mod.reshape(depth, 1, n))


def _phase_conv(buf, col0, w_ref, n_taps, r0, rows):
    phases = {}
    for k in range(n_taps):
        off = HALO + k - n_taps // 2
        phases.setdefault(off % SUBLANES, []).append((off - off % SUBLANES, k))
    out = None
    for p in sorted(phases):
        n_rows = rows if p == 0 else rows + SUBLANES
        z = None
        for aligned, k in phases[p]:
            xs = buf[r0 + aligned:r0 + aligned + n_rows, col0:col0 + D_CONV]
            term = (xs.reshape(n_rows // SUBLANES, SUBLANES, D_CONV) * w_ref[k]).reshape(n_rows, D_CONV)
            z = term if z is None else z + term
        z = z[p:p + rows]
        out = z if out is None else out + z
    return out


def _proj_kernel(tb, seq,
                 x_ref, xp_ref, xn_ref, sh_ref, sc_ref, g_ref, w_ref, rot_row_ref, rot_col_ref, gq_ref, gk_ref,
                 wa_ref, wc_ref, cb_ref, lg_ref, lb_ref,
                 y_ref, q_ref, kv_ref, cbuf, cos_ref, sin_ref):
    i = pl.program_id(1)
    has_prev = i > 0
    has_next = i < seq // tb - 1

    by_row = (lax.broadcasted_iota(jnp.int32, (1, LANES), 1) % HEAD_DIM) < HEAD_DIM // 2
    for r in range(tb // GRID_W):
        cos_ref[r * GRID_W:(r + 1) * GRID_W, :] = jnp.where(by_row, rot_row_ref[0, r:r + 1, :], rot_col_ref[0])
        sin_ref[r * GRID_W:(r + 1) * GRID_W, :] = jnp.where(by_row, rot_row_ref[1, r:r + 1, :], rot_col_ref[1])

    gain = g_ref[...] * (1.0 + sc_ref[0])

    def norm_mod(x):
        ms = jnp.mean(x * x, axis=-1, keepdims=True)
        return ((x * lax.rsqrt(ms + EPS)) * gain + sh_ref[0]).astype(BF16)

    lane = lax.broadcasted_iota(jnp.int32, (1, LANES), 1)
    rot_from_above = (lane % (HEAD_DIM // 2)) < (HEAD_DIM // 4)
    low_head = lane < HEAD_DIM
    r_i = lax.broadcasted_iota(jnp.int32, (LANES, LANES), 0) // HEAD_DIM
    c_i = lax.broadcasted_iota(jnp.int32, (LANES, LANES), 1) // HEAD_DIM
    head_ones = jnp.where(r_i == c_i, 1.0, 0.0).astype(BF16)

    def norm_rope(t, gain, scale, cos, sin):
        ssq = _dot((t * t).astype(BF16), head_ones)
        rs = lax.rsqrt(ssq * (1.0 / HEAD_DIM) + EPS) * scale
        tg = t * gain
        partner = jnp.where(rot_from_above, pltpu.roll(tg, LANES - HEAD_DIM // 4, 1),
                            pltpu.roll(tg, HEAD_DIM // 4, 1))
        return rs * (tg * cos + partner * sin)

    def conv_chunk(c):
        r0 = c * CONV_ROWS
        a = _phase_conv(cbuf, 0, wa_ref, SHORT_W, r0, CONV_ROWS)
        b_gate = cbuf[HALO + r0:HALO + r0 + CONV_ROWS, 2 * D_CONV:3 * D_CONV]
        y_ref[0, r0:r0 + CONV_ROWS, 0:D_CONV] = (b_gate * a).astype(BF16)
        acc = _phase_conv(cbuf, D_CONV, wc_ref, CONF_W, r0, CONV_ROWS) + cb_ref[...]
        mu = jnp.mean(acc, axis=-1, keepdims=True)
        xc = acc - mu
        var = jnp.mean(xc * xc, axis=-1, keepdims=True)
        yn = (xc * lax.rsqrt(var + EPS)) * lg_ref[...] + lb_ref[...]
        y_ref[0, r0:r0 + CONV_ROWS, D_CONV:2 * D_CONV] = (yn * _sigmoid(yn)).astype(BF16)

    half = tb // 2
    pieces = ((0, half + 2 * HALO), (half + 2 * HALO, tb + 2 * HALO))
    main = tuple((max(lo - HALO, 0), min(hi - HALO, tb)) for lo, hi in pieces)

    def normed(p):
        lo, hi = pieces[p]
        parts = [norm_mod(x_ref[0, main[p][0]:main[p][1], :])]
        if lo == 0:
            parts.insert(0, norm_mod(xp_ref[0]))
        if hi == tb + 2 * HALO:
            parts.append(norm_mod(xn_ref[0]))
        return jnp.concatenate(parts, axis=0)

    def main_rows(p, hb):
        return hb[main[p][0] + HALO - pieces[p][0]:main[p][1] + HALO - pieces[p][0]]

    def stage_short(p, hb):
        lo, hi = pieces[p]
        ua = _dot(hb, w_ref[:, 0:OFF_Q])
        cbuf[lo:hi, 0:D_CONV] = ua[:, 2 * D_CONV:3 * D_CONV] * ua[:, 0:D_CONV]
        cbuf[lo:hi, 2 * D_CONV:3 * D_CONV] = ua[:, D_CONV:2 * D_CONV]

    def stage_glu(p, hb):
        lo, hi = pieces[p]
        uc = _dot(hb, w_ref[:, OFF_C:D_IN])
        cbuf[lo:hi, D_CONV:2 * D_CONV] = uc[:, 0:D_CONV] * _sigmoid(uc[:, D_CONV:2 * D_CONV])
        if lo == 0:
            cbuf[0:HALO, 0:2 * D_CONV] = jnp.where(has_prev, cbuf[0:HALO, 0:2 * D_CONV], 0.0)
        if hi == tb + 2 * HALO:
            cbuf[hi - HALO:hi, 0:2 * D_CONV] = jnp.where(has_next, cbuf[hi - HALO:hi, 0:2 * D_CONV], 0.0)

    def q_outputs(p, hb):
        m_lo, m_hi = main[p]
        uq = _dot(main_rows(p, hb), w_ref[:, OFF_Q:OFF_K])
        for s in range(D_ATTN // LANES):
            qs = norm_rope(uq[:, s * LANES:(s + 1) * LANES], gq_ref[...], LOG2E * HEAD_DIM ** -0.5,
                           cos_ref[m_lo:m_hi, :], sin_ref[m_lo:m_hi, :])
            q_ref[0, m_lo:m_hi, s * LANES:(s + 1) * LANES] = qs.astype(BF16)

    def kv_outputs(p, hb):
        m_lo, m_hi = main[p]
        ukv = _dot(main_rows(p, hb), w_ref[:, OFF_K:OFF_C])
        kn = norm_rope(ukv[:, 0:LANES], gk_ref[...], 1.0, cos_ref[m_lo:m_hi, :], sin_ref[m_lo:m_hi, :])
        for j, t in enumerate((kn, ukv[:, LANES:2 * LANES])):
            swapped = pltpu.roll(t, HEAD_DIM, 1)
            kv_ref[0, m_lo:m_hi, (2 * j) * LANES:(2 * j + 1) * LANES] = jnp.where(low_head, t, swapped).astype(BF16)
            kv_ref[0, m_lo:m_hi, (2 * j + 1) * LANES:(2 * j + 2) * LANES] = (
                jnp.where(low_head, swapped, t).astype(BF16))

    for p, (lo, hi) in enumerate(pieces):
        hb = normed(p)
        stage_short(p, hb)
        stage_glu(p, hb)
        q_outputs(p, hb)
        kv_outputs(p, hb)
        for c in range(tb // CONV_ROWS):
            if lo < (c + 1) * CONV_ROWS + 2 * HALO <= hi:
                conv_chunk(c)


def _proj(x, mods, mod_row, norm_g, w_in_b, layer, rot_row, rot_col, gq, gk, conv_a_w, conv_c_w, conv_c_b, ln_g,
          ln_b, tb):
    b, n, _ = x.shape
    const = lambda bb, i: (0, 0)
    hb = tb // HALO
    kern = functools.partial(_proj_kernel, tb, n)
    return pl.pallas_call(
        kern,
        grid=(b, n // tb),
        in_specs=[
            pl.BlockSpec((1, tb, D_MODEL), lambda bb, i: (bb, i, 0)),
            pl.BlockSpec((1, HALO, D_MODEL), lambda bb, i: (bb, jnp.maximum(i * hb - 1, 0), 0)),
            pl.BlockSpec((1, HALO, D_MODEL), lambda bb, i: (bb, jnp.minimum((i + 1) * hb, n // HALO - 1), 0)),
            pl.BlockSpec((1, 1, D_MODEL), lambda bb, i: (mod_row(bb), 0, 0)),
            pl.BlockSpec((1, 1, D_MODEL), lambda bb, i: (mod_row(bb), 0, 1)),
            pl.BlockSpec((1, D_MODEL), const),
            pl.BlockSpec((None, D_MODEL, D_IN), lambda bb, i: (layer, 0, 0)),
            pl.BlockSpec((2, tb // GRID_W, LANES), lambda bb, i: (0, i, 0)),
            pl.BlockSpec((2, GRID_W, LANES), lambda bb, i: (0, 0, 0)),
            pl.BlockSpec((1, LANES), const),
            pl.BlockSpec((1, LANES), const),
            pl.BlockSpec((SHORT_W, SUBLANES, D_CONV), lambda bb, i: (0, 0, 0)),
            pl.BlockSpec((CONF_W, SUBLANES, D_CONV), lambda bb, i: (0, 0, 0)),
            pl.BlockSpec((1, D_CONV), const),
            pl.BlockSpec((1, D_CONV), const),
            pl.BlockSpec((1, D_CONV), const),
        ],
        out_specs=[
            pl.BlockSpec((1, tb, 2 * D_CONV), lambda bb, i: (bb, i, 0)),
            pl.BlockSpec((1, tb, D_ATTN), lambda bb, i: (bb, i, 0)),
            pl.BlockSpec((1, tb, 4 * LANES), lambda bb, i: (bb, i, 0)),
        ],
        out_shape=[
            jax.ShapeDtypeStruct((b, n, 2 * D_CONV), BF16),
            jax.ShapeDtypeStruct((b, n, D_ATTN), BF16),
            jax.ShapeDtypeStruct((b, n, 4 * LANES), BF16),
        ],
        scratch_shapes=[pltpu.VMEM((tb + 2 * HALO, 3 * D_CONV), F32),
                        pltpu.VMEM((tb, LANES), F32), pltpu.VMEM((tb, LANES), F32)],
        compiler_params=pltpu.CompilerParams(
            dimension_semantics=("parallel", "parallel"), vmem_limit_bytes=VMEM_LIMIT),
        name="proj",
    )(x, x, x, mods, mods, norm_g, w_in_b, rot_row, rot_col, gq, gk, conv_a_w, conv_c_w, conv_c_b, ln_g, ln_b)


def _mix_kernel(local, tb, seq,
                x_ref, y_ref, q_ref, kv_ref, kvp_ref, kvn_ref, ckv_ref, sink_ref, wo_ref, g1_ref,
                o_ref, kvbuf, ybuf):
    i = pl.program_id(1)
    lane = lax.broadcasted_iota(jnp.int32, (1, LANES), 1)
    low_head = lane < HEAD_DIM
    if local:
        kvbuf[0:WINDOW, :] = kvp_ref[0]
        kvbuf[WINDOW:WINDOW + tb, :] = kv_ref[0]
        kvbuf[WINDOW + tb:WINDOW + tb + WINDOW, :] = kvn_ref[0]
        n_loc = 3 * WINDOW
        rel = (lax.broadcasted_iota(jnp.int32, (WINDOW, WINDOW), 1)
               - lax.broadcasted_iota(jnp.int32, (WINDOW, WINDOW), 0))

    def pair_values(blocks):
        rows = []
        for keep_low in (True, False):
            ones = (jnp.where(low_head, 1.0, 0.0) if keep_low else jnp.where(low_head, 0.0, 1.0)).astype(BF16)
            for v in blocks:
                z = jnp.zeros_like(v)
                masked = jnp.where(low_head, v, z) if keep_low else jnp.where(low_head, z, v)
                rows.append(jnp.concatenate([masked, jnp.broadcast_to(ones, v.shape)], axis=1))
        return jnp.concatenate(rows, axis=0)

    def project(j):
        j0 = pl.multiple_of(j * WINDOW, WINDOW)
        out = _dot(ybuf[pl.ds(j0, WINDOW), :], wo_ref[...])
        o_ref[0, pl.ds(j0, WINDOW), :] = x_ref[0, pl.ds(j0, WINDOW), :] + g1_ref[0] * out

    def block_scores(j, kvh):
        j0 = j * WINDOW
        qj = q_ref[0, j0:j0 + WINDOW, :]
        parts = []
        for g in range(GROUP):
            h = kvh * GROUP + g
            slab = qj[:, (h // 2) * LANES:(h // 2 + 1) * LANES]
            parts.append(jnp.where(low_head if h % 2 == 0 else jnp.logical_not(low_head),
                                   slab, jnp.zeros_like(slab)))
        qs = jnp.concatenate(parts, axis=0)
        kc = ckv_ref[0, :, kvh * LANES:(kvh + 1) * LANES]
        vc = ckv_ref[0, :, (N_KV + kvh) * LANES:(N_KV + kvh + 1) * LANES]
        s_ctx = _dot_nt(qs, kc)
        s_loc = None
        if local:
            kw = kvbuf[j0:j0 + n_loc, kvh * LANES:(kvh + 1) * LANES]
            vw = kvbuf[j0:j0 + n_loc, (N_KV + kvh) * LANES:(N_KV + kvh + 1) * LANES]
            s_loc = _dot_nt(qs, kw)
            return s_ctx, s_loc, pair_values((vw, vc))
        return s_ctx, s_loc, pair_values((vc,))

    def mixer_rows(j, scores, fillers):
        j0 = j * WINDOW
        yj = y_ref[0, j0:j0 + WINDOW, :]
        y_parts = [yj[:, 0:D_CONV]]
        if local:
            base = i * tb + j0
            never = jnp.int32(2 * WINDOW)
            prev_ok = rel >= jnp.where(base >= WINDOW, 0, never)
            next_ok = rel <= jnp.where(base + 2 * WINDOW <= seq, 0, -never)
        if 0 in fillers:
            fillers[0]()
        n_pv = 0
        for kvh in range(N_KV):
            s_ctx, s_loc, v_pair = scores[kvh]
            for pair in range(GROUP // 2):
                probs, sink_terms = [], []
                for half in range(2):
                    g = 2 * pair + half
                    sk = sink_ref[kvh * GROUP + g] * LOG2E
                    sc = s_ctx[g * WINDOW:(g + 1) * WINDOW]
                    if local:
                        sg = s_loc[g * WINDOW:(g + 1) * WINDOW]
                        sl = jnp.concatenate([
                            jnp.where(prev_ok, sg[:, 0:WINDOW], NEG_INF),
                            sg[:, WINDOW:2 * WINDOW],
                            jnp.where(next_ok, sg[:, 2 * WINDOW:3 * WINDOW], NEG_INF)], axis=1)
                        m = jnp.maximum(jnp.maximum(jnp.max(sl, axis=-1, keepdims=True),
                                                    jnp.max(sc, axis=-1, keepdims=True)), sk)
                        probs.append(jnp.exp2(sl - m).astype(BF16))
                    else:
                        m = jnp.maximum(jnp.max(sc, axis=-1, keepdims=True), sk)
                    probs.append(jnp.exp2(sc - m).astype(BF16))
                    sink_terms.append(jnp.exp2(sk - m))
                o = _dot(jnp.concatenate(probs, axis=1), v_pair)
                n_pv += 1
                if n_pv in fillers:
                    fillers[n_pv]()
                den = o[:, LANES:2 * LANES] + jnp.where(low_head, sink_terms[0], sink_terms[1])
                y_parts.append((o[:, 0:LANES] / den).astype(BF16))
        y_parts.append(yj[:, D_CONV:2 * D_CONV])
        return jnp.concatenate(y_parts, axis=1)

    n_blocks = tb // WINDOW
    scores = [block_scores(0, kvh) for kvh in range(N_KV)]
    for j in range(n_blocks):
        nxt = []
        fillers = {}
        if j > 0:
            fillers[0] = functools.partial(project, j - 1)
        if j + 1 < n_blocks:
            fillers[GROUP // 2] = lambda j=j, nxt=nxt: nxt.append(block_scores(j + 1, 0))
            fillers[GROUP] = lambda j=j, nxt=nxt: nxt.append(block_scores(j + 1, 1))
        ybuf[j * WINDOW:(j + 1) * WINDOW, :] = mixer_rows(j, scores, fillers)
        scores = nxt
    project(n_blocks - 1)


def _mix(x, y, q, kv, ckv, mods, mod_row, sink, w_out_b, layer, tb, local):
    b, n, _ = x.shape
    n_ctx = ckv.shape[1]
    const = lambda bb, i: (0, 0)
    wb = tb // WINDOW
    kern = functools.partial(_mix_kernel, local, tb, n)
    return pl.pallas_call(
        kern,
        grid=(b, n // tb),
        in_specs=[
            pl.BlockSpec((1, tb, D_MODEL), lambda bb, i: (bb, i, 0)),
            pl.BlockSpec((1, tb, 2 * D_CONV), lambda bb, i: (bb, i, 0)),
            pl.BlockSpec((1, tb, D_ATTN), lambda bb, i: (bb, i, 0)),
            pl.BlockSpec((1, tb, 4 * LANES), lambda bb, i: (bb, i, 0)),
            pl.BlockSpec((1, WINDOW, 4 * LANES), lambda bb, i: (bb, jnp.maximum(i * wb - 1, 0), 0)),
            pl.BlockSpec((1, WINDOW, 4 * LANES),
                         lambda bb, i: (bb, jnp.minimum((i + 1) * wb, n // WINDOW - 1), 0)),
            pl.BlockSpec((1, n_ctx, 4 * LANES), lambda bb, i: (bb, 0, 0)),
            pl.BlockSpec(memory_space=pltpu.SMEM),
            pl.BlockSpec((None, D_MODEL, D_MODEL), lambda bb, i: (layer, 0, 0)),
            pl.BlockSpec((1, 1, D_MODEL), lambda bb, i: (mod_row(bb), 0, 2)),
        ],
        out_specs=pl.BlockSpec((1, tb, D_MODEL), lambda bb, i: (bb, i, 0)),
        out_shape=jax.ShapeDtypeStruct((b, n, D_MODEL), F32),
        scratch_shapes=[
            pltpu.VMEM((tb + 2 * WINDOW, 4 * LANES), BF16),
            pltpu.VMEM((tb, D_MODEL), BF16),
        ],
        compiler_params=pltpu.CompilerParams(
            dimension_semantics=("parallel", "parallel"), vmem_limit_bytes=VMEM_LIMIT),
        name="mix_local" if local else "mix_ctx",
    )(x, y, q, kv, kv, kv, ckv, sink, w_out_b, mods)


def _mlp_kernel(x_ref, sh_ref, sc_ref, gate_ref, g_ref, w1_ref, w2_ref, o_ref):
    x = x_ref[0]
    ms = jnp.mean(x * x, axis=-1, keepdims=True)
    h = (x * lax.rsqrt(ms + EPS)) * g_ref[...]
    h = h * (1.0 + sc_ref[0]) + sh_ref[0]
    hb = h.astype(BF16)
    acc = None
    for c in range(D_FF // FF_CHUNK):
        hid = _dot(hb, w1_ref[:, c * FF_CHUNK:(c + 1) * FF_CHUNK])
        r = jnp.maximum(hid, 0.0)
        part = _dot((r * r).astype(BF16), w2_ref[c * FF_CHUNK:(c + 1) * FF_CHUNK, :])
        acc = part if acc is None else acc + part
    o_ref[0] = x + gate_ref[0] * acc


def _mlp(x, mods, mod_row, norm_g, w1_b, w2_b, layer, tb):
    b, n, _ = x.shape
    const = lambda bb, i: (0, 0)
    return pl.pallas_call(
        _mlp_kernel,
        grid=(b, n // tb),
        in_specs=[
            pl.BlockSpec((1, tb, D_MODEL), lambda bb, i: (bb, i, 0)),
            pl.BlockSpec((1, 1, D_MODEL), lambda bb, i: (mod_row(bb), 0, 3)),
            pl.BlockSpec((1, 1, D_MODEL), lambda bb, i: (mod_row(bb), 0, 4)),
            pl.BlockSpec((1, 1, D_MODEL), lambda bb, i: (mod_row(bb), 0, 5)),
            pl.BlockSpec((1, D_MODEL), const),
            pl.BlockSpec((None, D_MODEL, D_FF), lambda bb, i: (layer, 0, 0), pipeline_mode=pl.Buffered(1)),
            pl.BlockSpec((None, D_FF, D_MODEL), lambda bb, i: (layer, 0, 0), pipeline_mode=pl.Buffered(1)),
        ],
        out_specs=pl.BlockSpec((1, tb, D_MODEL), lambda bb, i: (bb, i, 0)),
        out_shape=jax.ShapeDtypeStruct((b, n, D_MODEL), F32),
        compiler_params=pltpu.CompilerParams(
            dimension_semantics=("parallel", "parallel"), vmem_limit_bytes=VMEM_LIMIT),
        name="mlp",
    )(x, mods, mods, mods, norm_g, w1_b, w2_b)


def _rope_tables(seq):
    d_axis = HEAD_DIM // 2
    inv_freq = ROPE_BASE ** (-jnp.arange(0, d_axis, 2, dtype=F32) / d_axis)
    reps = LANES // HEAD_DIM

    def table(n_pos, first):
        ang = jnp.arange(n_pos, dtype=jnp.int32).astype(F32)[:, None] * inv_freq[None, :]
        zero = jnp.zeros_like(ang)
        cos = [jnp.cos(ang), jnp.cos(ang), zero, zero]
        sin = [-jnp.sin(ang), jnp.sin(ang), zero, zero]
        if not first:
            cos, sin = cos[2:] + cos[:2], sin[2:] + sin[:2]
        return jnp.stack([jnp.tile(jnp.concatenate(cos, axis=-1), (1, reps)),
                          jnp.tile(jnp.concatenate(sin, axis=-1), (1, reps))])

    return table(seq // GRID_W, True), table(GRID_W, False)


def kernel(x, c, ctx, c_ctx, w_mod, b_mod, norm1_g, w_in, conv_a_w, q_norm_g, k_norm_g, attn_sink,
           conv_c_w, conv_c_b, ln_c_g, ln_c_b, w_out, norm2_g, w_mlp1, w_mlp2):
    batch, seq, _ = x.shape
    n_ctx = ctx.shape[1]
    depth = w_mod.shape[0]
    tb = min(TOKEN_BLOCK, seq)
    tb_proj = min(PROJ_BLOCK, seq)
    tb_ctx = n_ctx
    ctx_row = batch
    assert x.shape[2] == D_MODEL and batch + 1 <= SUBLANES and HALO >= CONF_W // 2
    assert seq % tb == 0 and tb % WINDOW == 0 and seq % tb_proj == 0 and tb_proj % GRID_W == 0
    assert n_ctx % WINDOW == 0 and n_ctx % GRID_W == 0
    assert (tb_proj // 2) % CONV_ROWS == 0 and (tb_ctx // 2) % CONV_ROWS == 0

    cvec = jnp.concatenate([c, c_ctx[None, :], jnp.zeros((SUBLANES - batch - 1, D_MODEL), F32)], axis=0)
    mods_all = _modulation(cvec.T, batch + 1, w_mod, b_mod).reshape(depth, SUBLANES, 1, 6 * D_MODEL)

    rot_row, rot_col = _rope_tables(seq)
    rot_row_c = jnp.stack([jnp.ones((n_ctx // GRID_W, LANES), F32), jnp.zeros((n_ctx // GRID_W, LANES), F32)])
    rot_col_c = jnp.stack([jnp.ones((GRID_W, LANES), F32), jnp.zeros((GRID_W, LANES), F32)])
    reps = LANES // HEAD_DIM
    lat_row = lambda bb: bb
    cx_row = lambda bb: ctx_row

    w_in_b = w_in.astype(BF16)
    w_out_b = w_out.astype(BF16)
    w1_b = w_mlp1.astype(BF16)
    w2_b = w_mlp2.astype(BF16)

    xc = ctx
    for l in range(depth):
        last = l == depth - 1
        mods = mods_all[l]
        g1n = norm1_g[l][None, :]
        g2n = norm2_g[l][None, :]
        gq = jnp.tile(q_norm_g[l], reps)[None, :]
        gk = jnp.tile(k_norm_g[l], reps)[None, :]
        conv_args = (jnp.broadcast_to(conv_a_w[l][:, None, :], (SHORT_W, SUBLANES, D_CONV)),
                     jnp.broadcast_to(conv_c_w[l][:, None, :], (CONF_W, SUBLANES, D_CONV)),
                     conv_c_b[l][None, :], ln_c_g[l][None, :], ln_c_b[l][None, :])

        y_c, q_c, kv_c = _proj(xc, mods, cx_row, g1n, w_in_b, l, rot_row_c, rot_col_c, gq, gk, *conv_args, tb_ctx)
        y, q, kv = _proj(x, mods, lat_row, g1n, w_in_b, l, rot_row, rot_col, gq, gk, *conv_args, tb_proj)
        x = _mix(x, y, q, kv, kv_c, mods, lat_row, attn_sink[l], w_out_b, l, tb, True)
        x = _mlp(x, mods, lat_row, g2n, w1_b, w2_b, l, tb)
        if not last:
            xc = _mix(xc, y_c, q_c, kv_c, kv_c, mods, cx_row, attn_sink[l], w_out_b, l, tb_ctx, False)
            xc = _mlp(xc, mods, cx_row, g2n, w1_b, w2_b, l, tb_ctx)
    return x
```
